```python
import math
import jax
import jax.numpy as jnp
from jax import lax
import numpy as np

D_MODEL = 2048
BATCH = 4
SEQ = 2048
DEPTH = 2
DEC_BATCH = 128
DEC_SEQ = 8
PAST_LEN = 2048
PAGE_SIZE = 128

HEAD_DIM = 128
ROPE_THETA = 10000.0
NORM_EPS = 1e-6
A_GROUPS = ((128, 1), (512, 4), (2048, 16))
A_HEADS = 8
B_HEADS = 16
B_KV_HEADS = 4
N_BRANCH = 3
CMP_LEN = 32
CMP_STRIDE = 16
CMP_HIDDEN = 256
SLC_LEN = 64
N_SELECT = 16
B_WINDOW = 512
D_FF = 5632
N_EXPERTS = 8
TOP_K = 2
D_FF_EXPERT = 7168
N_A_LAYERS = DEPTH // 2
N_B_LAYERS = DEPTH - N_A_LAYERS
N_DENSE = (DEPTH + 1) // 2
N_MOE = DEPTH // 2
Q_BLOCK = 128
NEG_INF = -1e30
FORCE_SCORE = 1e9

kernel_name = 'yoco_dilated_nsa_conditioned_decoder_step'


def rmsnorm(x, g):
    xf = x.astype(jnp.float32)
    y = xf * lax.rsqrt(jnp.mean(xf * xf, axis=-1, keepdims=True) + NORM_EPS)
    return (y * g.astype(jnp.float32)).astype(x.dtype)


def modulate(x, shift, scale):
    return x * (1 + scale[:, None, :]) + shift[:, None, :]


def rope(x, pos):
    half = x.shape[-1] // 2
    inv = ROPE_THETA ** (-jnp.arange(half, dtype=jnp.float32) / half)
    ang = pos.astype(jnp.float32)[:, None] * inv[None, :]
    cos = jnp.cos(ang)[:, None, :]
    sin = jnp.sin(ang)[:, None, :]
    xf = x.astype(jnp.float32)
    x1, x2 = xf[..., :half], xf[..., half:]
    return jnp.concatenate([x1 * cos - x2 * sin, x2 * cos + x1 * sin], axis=-1).astype(x.dtype)


def masked_softmax(s, mask):
    s = jnp.where(mask, s, NEG_INF)
    m = jnp.max(s, axis=-1, keepdims=True)
    e = jnp.where(mask, jnp.exp(s - m), 0.0)
    den = jnp.sum(e, axis=-1, keepdims=True)
    p = e / jnp.maximum(den, 1e-30)
    lse = (m + jnp.log(jnp.maximum(den, 1e-30)))[..., 0]
    return p, lse


def band_attn(q, k, v, n_back):
    B, L, Hq, d = q.shape
    Hkv = k.shape[2]
    hpg = Hq // Hkv
    qb = math.gcd(L, Q_BLOCK)
    nb = L // qb
    nk = n_back + qb
    kp = jnp.pad(k, ((0, 0), (n_back, 0), (0, 0), (0, 0)))
    vp = jnp.pad(v, ((0, 0), (n_back, 0), (0, 0), (0, 0)))
    qg = jnp.moveaxis(q.reshape(B, nb, qb, Hkv, hpg, d), 1, 0)
    rel = jnp.arange(qb)[:, None] + n_back - jnp.arange(nk)[None, :]
    scale = d ** -0.5

    def block(args):
        i, qi = args
        ki = lax.dynamic_slice_in_dim(kp, i * qb, nk, axis=1)
        vi = lax.dynamic_slice_in_dim(vp, i * qb, nk, axis=1)
        s = jnp.einsum('bqgjd,bkgd->bgjqk', qi, ki).astype(jnp.float32) * scale
        kpos = i * qb - n_back + jnp.arange(nk)
        mask = (rel >= 0) & (rel <= n_back) & (kpos >= 0)[None, :]
        p, lse = masked_softmax(s, mask)
        o = jnp.einsum('bgjqk,bkgd->bqgjd', p.astype(v.dtype), vi)
        return o.reshape(B, qb, Hq, d), lse.transpose(0, 3, 1, 2).reshape(B, qb, Hq)

    o, lse = lax.map(block, (jnp.arange(nb), qg))
    o = jnp.moveaxis(o, 0, 1).reshape(B, L, Hq, d)
    lse = jnp.moveaxis(lse, 0, 1).reshape(B, L, Hq)
    return o, lse


def dilated_prompt(q, k, v, dil, n_back):
    B, L, H, d = q.shape

    def to_res(x):
        return x.reshape(B, L // dil, dil, H, d).transpose(0, 2, 1, 3, 4).reshape(B * dil, L // dil, H, d)

    o, lse = band_attn(to_res(q), to_res(k), to_res(v), n_back)
    o = o.reshape(B, dil, L // dil, H, d).transpose(0, 2, 1, 3, 4).reshape(B, L, H, d)
    lse = lse.reshape(B, dil, L // dil, H).transpose(0, 2, 1, 3).reshape(B, L, H)
    return o, lse


def dilated_sample(q, k_all, v_all, dil, n_back):
    B, Q, H, d = q.shape
    Lk = k_all.shape[1]
    idx = (Lk - Q + jnp.arange(Q))[:, None] - dil * jnp.arange(n_back + 1)[None, :]
    valid = idx >= 0
    idx = jnp.maximum(idx, 0)
    kg = k_all[:, idx]
    vg = v_all[:, idx]
    s = jnp.einsum('bqhd,bqjhd->bhqj', q, kg).astype(jnp.float32) * (d ** -0.5)
    p, lse = masked_softmax(s, valid[None, None])
    o = jnp.einsum('bhqj,bqjhd->bqhd', p.astype(v_all.dtype), vg)
    return o, lse.transpose(0, 2, 1)


def mixer_a(xn, pos, w_qkv, w_o, bufs):
    B, L, _ = xn.shape
    qkv = (xn @ w_qkv).reshape(B, L, len(A_GROUPS), 3, A_HEADS, HEAD_DIM)
    outs, lses, states = [], [], []
    for g, (window, dil) in enumerate(A_GROUPS):
        q = rope(qkv[:, :, g, 0], pos)
        kv_new = jnp.stack([rope(qkv[:, :, g, 1], pos), qkv[:, :, g, 2]], axis=2)
        n_back = window // dil
        if bufs is None:
            kv_all = kv_new
            o, lse = dilated_prompt(q, kv_new[:, :, 0], kv_new[:, :, 1], dil, n_back)
        else:
            kv_all = jnp.concatenate([bufs[g], kv_new], axis=1)
            o, lse = dilated_sample(q, kv_all[:, :, 0], kv_all[:, :, 1], dil, n_back)
        outs.append(o)
        lses.append(lse)
        n_keep = min(window, kv_all.shape[1])
        states.append(kv_all[:, kv_all.shape[1] - n_keep:])
    wts = jax.nn.softmax(jnp.stack(lses, axis=0), axis=0)
    o = jnp.einsum('gblh,gblhd->blhd', wts.astype(xn.dtype), jnp.stack(outs, axis=0))
    return o.reshape(B, L, A_HEADS * HEAD_DIM) @ w_o, states


def compress(x, w1, w2, pe):
    B, L, H, d = x.shape
    n_cmp = (L - CMP_LEN) // CMP_STRIDE + 1
    n_chunk = n_cmp + 1
    chunks = x[:, :n_chunk * CMP_STRIDE].reshape(B, n_chunk, CMP_STRIDE, H, d)
    chunks = chunks.transpose(0, 1, 3, 2, 4).reshape(B, n_chunk, H, CMP_STRIDE * d)
    half = CMP_STRIDE * d
    hid = chunks[:, :-1] @ w1[:half] + chunks[:, 1:] @ w1[half:] + pe.reshape(-1) @ w1
    return jax.nn.silu(hid) @ w2


def overlap_matrix(n_cmp, n_slc):
    c0 = np.arange(n_cmp) * CMP_STRIDE
    s0 = np.arange(n_slc) * SLC_LEN
    m = (c0[:, None] < s0[None, :] + SLC_LEN) & (c0[:, None] + CMP_LEN > s0[None, :])
    return jnp.asarray(m.astype(np.float32))


def branch_cmp(q, kc, vc, q_pos):
    B, Q, Hq, d = q.shape
    Hkv = kc.shape[2]
    n_cmp = kc.shape[1]
    qg = q.reshape(B, Q, Hkv, Hq // Hkv, d)
    end = jnp.arange(n_cmp) * CMP_STRIDE + CMP_LEN - 1
    s = jnp.einsum('bqgjd,bngd->bgjqn', qg, kc).astype(jnp.float32) * (d ** -0.5)
    p, _ = masked_softmax(s, end[None, :] <= q_pos[:, None])
    o = jnp.einsum('bgjqn,bngd->bqgjd', p.astype(vc.dtype), vc).reshape(B, Q, Hq, d)
    return o, jnp.sum(p, axis=2)


def select_blocks(p_slc, q_pos, n_slc):
    blk = jnp.arange(n_slc)[None, :]
    cur = (q_pos // SLC_LEN)[:, None]
    forced = (blk == 0) | (blk == cur) | (blk == cur - 1)
    imp = jnp.where(forced, FORCE_SCORE, p_slc)
    imp = jnp.where(blk <= cur, imp, NEG_INF)
    _, idx = lax.top_k(imp, min(N_SELECT, n_slc))
    return idx


def branch_slc(q, k, v, q_pos, idx):
    B, Q, Hq, d = q.shape
    L, Hkv = k.shape[1], k.shape[2]
    hpg = Hq // Hkv
    n_sel = idx.shape[-1]
    n_slc = -(-L // SLC_LEN)
    pad = n_slc * SLC_LEN - L
    kb = jnp.pad(k, ((0, 0), (0, pad), (0, 0), (0, 0))).reshape(B, n_slc, SLC_LEN, Hkv, d).transpose(0, 3, 1, 2, 4)
    vb = jnp.pad(v, ((0, 0), (0, pad), (0, 0), (0, 0))).reshape(B, n_slc, SLC_LEN, Hkv, d).transpose(0, 3, 1, 2, 4)
    qb = math.gcd(Q, Q_BLOCK)
    nb = Q // qb
    qg = q.reshape(B, nb, qb, Hkv, hpg, d)
    idx_b = idx.transpose(0, 2, 1, 3).reshape(B, nb, qb, Hkv, n_sel)
    pos_b = q_pos.reshape(nb, qb)
    g_arr = jnp.arange(Hkv)[None, :, None]
    scale = d ** -0.5

    def per_seq(args):
        qs, ids, ks, vs = args

        def per_block(a):
            qi, ii, pi = a
            kg = ks[g_arr, ii]
            vg = vs[g_arr, ii]
            s = jnp.einsum('qgjd,qgnsd->qgjns', qi, kg).astype(jnp.float32) * scale
            s = s.reshape(qb, Hkv, hpg, n_sel * SLC_LEN)
            kpos = ii[..., None] * SLC_LEN + jnp.arange(SLC_LEN)
            mask = (kpos <= pi[:, None, None, None]).reshape(qb, Hkv, 1, n_sel * SLC_LEN)
            p, _ = masked_softmax(s, mask)
            o = jnp.einsum('qgjk,qgkd->qgjd', p.astype(vs.dtype), vg.reshape(qb, Hkv, n_sel * SLC_LEN, d))
            return o.reshape(qb, Hq, d)

        return lax.map(per_block, (qs, ids, pos_b))

    o = lax.map(per_seq, (qg, idx_b, kb, vb))
    return o.reshape(B, Q, Hq, d)


def window_dense(q, k, v, q_pos, k_pos, n_back):
    B, Q, Hq, d = q.shape
    Hkv = k.shape[2]
    qg = q.reshape(B, Q, Hkv, Hq // Hkv, d)
    s = jnp.einsum('bqgjd,bkgd->bgjqk', qg, k).astype(jnp.float32) * (d ** -0.5)
    rel = q_pos[:, None] - k_pos[None, :]
    p, _ = masked_softmax(s, (rel >= 0) & (rel <= n_back))
    o = jnp.einsum('bgjqk,bkgd->bqgjd', p.astype(v.dtype), v)
    return o.reshape(B, Q, Hq, d)


def build_b_context(h, c, start, b_past, b_win_buf, w):
    B, L, _ = h.shape
    pos = start + jnp.arange(L, dtype=jnp.int32)
    shift, scale = jnp.split(jax.nn.silu(c) @ w['kv_mod_w'] + w['kv_mod_b'], 2, axis=-1)
    z = modulate(rmsnorm(h, w['kv_norm_g']), shift, scale)
    kv = (z @ w['b_w_kv']).reshape(B, L, 2 * N_BRANCH, B_KV_HEADS, HEAD_DIM)
    rows = jnp.stack([kv[:, :, 0], kv[:, :, 1], rope(kv[:, :, 2], pos), kv[:, :, 3]], axis=2)
    win_rows = jnp.stack([rope(kv[:, :, 4], pos), kv[:, :, 5]], axis=2)
    if b_past is None:
        full, win_all = rows, win_rows
    else:
        full = jnp.concatenate([b_past, rows], axis=1)
        win_all = jnp.concatenate([b_win_buf, win_rows], axis=1)
    n_win = win_all.shape[1]
    win_k_pos = start + L - n_win + jnp.arange(n_win, dtype=jnp.int32)
    kc = compress(full[:, :, 0], w['cmp_w1_k'], w['cmp_w2_k'], w['cmp_pe_k'])
    kc = rope(kc, jnp.arange(kc.shape[1], dtype=jnp.int32) * CMP_STRIDE + CMP_LEN - 1)
    vc = compress(full[:, :, 1], w['cmp_w1_v'], w['cmp_w2_v'], w['cmp_pe_v'])
    ctx = {'kc': kc, 'vc': vc, 'k_slc': full[:, :, 2], 'v_slc': full[:, :, 3],
           'k_win': win_all[:, :, 0], 'v_win': win_all[:, :, 1], 'win_k_pos': win_k_pos,
           'prompt': b_past is None}
    new_win = win_all[:, n_win - min(B_WINDOW, n_win):]
    return ctx, rows, new_win


def mixer_b(xn, pos, ctx, w_qg, w_o):
    B, L, _ = xn.shape
    qg = xn @ w_qg
    q = rope(qg[..., :B_HEADS * HEAD_DIM].reshape(B, L, B_HEADS, HEAD_DIM), pos)
    gates = jax.nn.sigmoid(qg[..., B_HEADS * HEAD_DIM:].astype(jnp.float32))
    gates = gates.reshape(B, L, B_HEADS, N_BRANCH).astype(xn.dtype)
    o_cmp, p_grp = branch_cmp(q, ctx['kc'], ctx['vc'], pos)
    n_slc = -(-ctx['k_slc'].shape[1] // SLC_LEN)
    p_slc = jnp.einsum('bgqn,ns->bgqs', p_grp, overlap_matrix(ctx['kc'].shape[1], n_slc))
    idx = select_blocks(p_slc, pos, n_slc)
    o_slc = branch_slc(q, ctx['k_slc'], ctx['v_slc'], pos, idx)
    if ctx['prompt']:
        o_win, _ = band_attn(q, ctx['k_win'], ctx['v_win'], B_WINDOW)
    else:
        o_win = window_dense(q, ctx['k_win'], ctx['v_win'], pos, ctx['win_k_pos'], B_WINDOW)
    o = gates[..., 0:1] * o_cmp + gates[..., 1:2] * o_slc + gates[..., 2:3] * o_win
    return o.reshape(B, L, B_HEADS * HEAD_DIM) @ w_o


def swiglu(x, w_in, w_out):
    hcat = x @ w_in
    return (jax.nn.silu(hcat[..., :D_FF]) * hcat[..., D_FF:]) @ w_out


def moe(x, w_router, b_router, w_in, w_out):
    B, L, D = x.shape
    t = x.reshape(B * L, D)
    logits = (t @ w_router).astype(jnp.float32) + b_router.astype(jnp.float32)
    top_v, top_i = lax.top_k(logits, TOP_K)
    top_w = jax.nn.softmax(top_v, axis=-1)
    gate = jnp.sum(jax.nn.one_hot(top_i, N_EXPERTS, dtype=jnp.float32) * top_w[..., None], axis=1).astype(x.dtype)
    y = jnp.zeros_like(t)
    for e in range(N_EXPERTS):
        hcat = t @ w_in[e]
        y = y + gate[:, e:e + 1] * ((jax.nn.silu(hcat[:, :D_FF_EXPERT]) * hcat[:, D_FF_EXPERT:]) @ w_out[e])
    return y.reshape(B, L, D)


def trunk(x, c, start, a_bufs, b_past, b_win_buf, w):
    B, L, _ = x.shape
    pos = start + jnp.arange(L, dtype=jnp.int32)
    h = x
    a_states = []
    ctx = None
    b_rows = None
    b_win_state = None
    for layer in range(DEPTH):
        mod = jax.nn.silu(c) @ w['mod_w'][layer] + w['mod_b'][layer]
        sh_m, sc_m, gt_m, sh_f, sc_f, gt_f = jnp.split(mod, 6, axis=-1)
        xn = modulate(rmsnorm(h, w['norm_mix_g'][layer]), sh_m, sc_m)
        if layer < N_A_LAYERS:
            bufs = None if a_bufs is None else [buf[layer] for buf in a_bufs]
            y, st = mixer_a(xn, pos, w['a_w_qkv'][layer], w['a_w_o'][layer], bufs)
            a_states.append(st)
        else:
            if ctx is None:
                ctx, b_rows, b_win_state = build_b_context(h, c, start, b_past, b_win_buf, w)
            ib = layer - N_A_LAYERS
            y = mixer_b(xn, pos, ctx, w['b_w_qg'][ib], w['b_w_o'][ib])
        h = h + gt_m[:, None, :] * y
        xn = modulate(rmsnorm(h, w['norm_ffn_g'][layer]), sh_f, sc_f)
        if layer % 2 == 0:
            f = swiglu(xn, w['ffn_w_in'][layer // 2], w['ffn_w_out'][layer // 2])
        else:
            f = moe(xn, w['moe_w_router'][layer // 2], w['moe_b_router'][layer // 2],
                    w['moe_w_in'][layer // 2], w['moe_w_out'][layer // 2])
        h = h + gt_f[:, None, :] * f
    y = rmsnorm(h, w['final_norm_g'])
    a_new = [jnp.stack([st[g] for st in a_states], axis=0) for g in range(len(A_GROUPS))]
    return y, a_new, b_rows, b_win_state


def setup_inputs(seed: int = 0) -> dict:
    key = jax.random.key(seed)
    keys = iter(jax.random.split(key, 48))

    def nrm(shape, scale=1.0):
        return jax.random.normal(next(keys), shape, jnp.float32) * scale

    n_pages = PAST_LEN // PAGE_SIZE
    n_used = DEC_BATCH * n_pages
    n_pool = n_used + n_used // 4
    hd = HEAD_DIM
    D = D_MODEL
    inp = {}
    inp['x_prompt'] = nrm((BATCH, SEQ, D))
    inp['x_sample'] = nrm((DEC_BATCH, DEC_SEQ, D))
    inp['c_prompt'] = nrm((BATCH, D))
    inp['c_sample'] = nrm((DEC_BATCH, D))
    for window, _ in A_GROUPS:
        inp['cache_a_w%d' % window] = nrm((N_A_LAYERS, DEC_BATCH, min(window, PAST_LEN), 2, A_HEADS, hd))
    inp['cache_b_kv'] = nrm((n_pool, PAGE_SIZE, 4, B_KV_HEADS, hd))
    inp['cache_b_win'] = nrm((DEC_BATCH, min(B_WINDOW, PAST_LEN), 2, B_KV_HEADS, hd))
    inp['page_table'] = jax.random.permutation(next(keys), n_pool)[:n_used].reshape(DEC_BATCH, n_pages).astype(jnp.int32)
    inp['norm_mix_g'] = 1.0 + nrm((DEPTH, D), 0.02)
    inp['norm_ffn_g'] = 1.0 + nrm((DEPTH, D), 0.02)
    inp['mod_w'] = nrm((DEPTH, D, 6 * D), 0.5 * D ** -0.5)
    inp['mod_b'] = nrm((DEPTH, 6 * D), 0.02)
    inp['a_w_qkv'] = nrm((N_A_LAYERS, D, len(A_GROUPS) * 3 * A_HEADS * hd), D ** -0.5)
    inp['a_w_o'] = nrm((N_A_LAYERS, A_HEADS * hd, D), (A_HEADS * hd) ** -0.5)
    inp['kv_norm_g'] = 1.0 + nrm((D,), 0.02)
    inp['kv_mod_w'] = nrm((D, 2 * D), 0.5 * D ** -0.5)
    inp['kv_mod_b'] = nrm((2 * D,), 0.02)
    inp['b_w_kv'] = nrm((D, 2 * N_BRANCH * B_KV_HEADS * hd), D ** -0.5)
    inp['cmp_w1_k'] = nrm((CMP_LEN * hd, CMP_HIDDEN), (CMP_LEN * hd) ** -0.5)
    inp['cmp_w2_k'] = nrm((CMP_HIDDEN, hd), CMP_HIDDEN ** -0.5)
    inp['cmp_pe_k'] = nrm((CMP_LEN, hd), 0.1)
    inp['cmp_w1_v'] = nrm((CMP_LEN * hd, CMP_HIDDEN), (CMP_LEN * hd) ** -0.5)
    inp['cmp_w2_v'] = nrm((CMP_HIDDEN, hd), CMP_HIDDEN ** -0.5)
    inp['cmp_pe_v'] = nrm((CMP_LEN, hd), 0.1)
    inp['b_w_qg'] = nrm((N_B_LAYERS, D, B_HEADS * hd + N_BRANCH * B_HEADS), D ** -0.5)
    inp['b_w_o'] = nrm((N_B_LAYERS, B_HEADS * hd, D), (B_HEADS * hd) ** -0.5)
    inp['ffn_w_in'] = nrm((N_DENSE, D, 2 * D_FF), D ** -0.5)
    inp['ffn_w_out'] = nrm((N_DENSE, D_FF, D), D_FF ** -0.5)
    inp['moe_w_router'] = nrm((N_MOE, D, N_EXPERTS), D ** -0.5)
    inp['moe_b_router'] = nrm((N_MOE, N_EXPERTS), 0.01)
    inp['moe_w_in'] = nrm((N_MOE, N_EXPERTS, D, 2 * D_FF_EXPERT), D ** -0.5)
    inp['moe_w_out'] = nrm((N_MOE, N_EXPERTS, D_FF_EXPERT, D), D_FF_EXPERT ** -0.5)
    inp['final_norm_g'] = 1.0 + nrm((D,), 0.02)
    return inp


def reference(x_prompt, x_sample, c_prompt, c_sample, cache_a_w128, cache_a_w512, cache_a_w2048,
              cache_b_kv, cache_b_win, page_table, norm_mix_g, norm_ffn_g, mod_w, mod_b,
              a_w_qkv, a_w_o, kv_norm_g, kv_mod_w, kv_mod_b, b_w_kv,
              cmp_w1_k, cmp_w2_k, cmp_pe_k, cmp_w1_v, cmp_w2_v, cmp_pe_v,
              b_w_qg, b_w_o, ffn_w_in, ffn_w_out, moe_w_router, moe_b_router,
              moe_w_in, moe_w_out, final_norm_g):
    w = {'norm_mix_g': norm_mix_g, 'norm_ffn_g': norm_ffn_g, 'mod_w': mod_w, 'mod_b': mod_b,
         'a_w_qkv': a_w_qkv, 'a_w_o': a_w_o, 'kv_norm_g': kv_norm_g, 'kv_mod_w': kv_mod_w,
         'kv_mod_b': kv_mod_b, 'b_w_kv': b_w_kv, 'cmp_w1_k': cmp_w1_k, 'cmp_w2_k': cmp_w2_k,
         'cmp_pe_k': cmp_pe_k, 'cmp_w1_v': cmp_w1_v, 'cmp_w2_v': cmp_w2_v, 'cmp_pe_v': cmp_pe_v,
         'b_w_qg': b_w_qg, 'b_w_o': b_w_o, 'ffn_w_in': ffn_w_in, 'ffn_w_out': ffn_w_out,
         'moe_w_router': moe_w_router, 'moe_b_router': moe_b_router, 'moe_w_in': moe_w_in,
         'moe_w_out': moe_w_out, 'final_norm_g': final_norm_g}
    y_prompt, a_prompt, bkv_prompt, bwin_prompt = trunk(x_prompt, c_prompt, 0, None, None, None, w)
    n_pages = page_table.shape[1]
    past = cache_b_kv[page_table].reshape(x_sample.shape[0], n_pages * PAGE_SIZE, *cache_b_kv.shape[2:])
    y_sample, a_sample, bkv_sample, bwin_sample = trunk(
        x_sample, c_sample, PAST_LEN, (cache_a_w128, cache_a_w512, cache_a_w2048), past, cache_b_win, w)
    a128_prompt, a512_prompt, a2048_prompt = a_prompt
    a128_sample, a512_sample, a2048_sample = a_sample
    return (y_prompt, y_sample, a128_prompt, a128_sample, a512_prompt, a512_sample,
            a2048_prompt, a2048_sample, bkv_prompt, bkv_sample, bwin_prompt, bwin_sample)
```

```python
import functools
import math

import numpy as np
import jax
import jax.numpy as jnp
from jax import lax
from jax.experimental import pallas as pl
from jax.experimental.pallas import tpu as pltpu

F32 = jnp.float32
BF16 = jnp.bfloat16

D_MODEL = 2048
HEAD_DIM = 128
ROPE_THETA = 10000.0
NORM_EPS = 1e-6
A_GROUPS = ((128, 1), (512, 4), (2048, 16))
A_HEADS = 8
A_COLS = A_HEADS * HEAD_DIM
B_HEADS = 16
B_KV_HEADS = 4
B_HPG = B_HEADS // B_KV_HEADS
N_BRANCH = 3
CMP_LEN = 32
CMP_STRIDE = 16
CMP_HIDDEN = 256
SLC_LEN = 64
N_SELECT = 16
B_WINDOW = 512
D_FF = 5632
N_EXPERTS = 8
TOP_K = 2
D_FF_EXPERT = 7168
PAGE_SIZE = 128
Q_BLOCK = 128
NEG_INF = -1e30
FORCE_SCORE = 1e9
LANES = 128
SUBLANES = 8
VMEM_LIMIT = 56 * 1024 * 1024
ATT_SCALE = HEAD_DIM ** -0.5


def _cparams(n_axes, vmem=VMEM_LIMIT):
    return pltpu.CompilerParams(dimension_semantics=("arbitrary",) * n_axes, vmem_limit_bytes=vmem)


def _silu(x):
    return x * jax.nn.sigmoid(x)


def _masked_softmax(s, mask):
    s = jnp.where(mask, s, NEG_INF)
    m = jnp.max(s, axis=-1, keepdims=True)
    e = jnp.where(mask, jnp.exp(s - m), 0.0)
    den = jnp.maximum(jnp.sum(e, axis=-1, keepdims=True), 1e-30)
    return e / den, m + jnp.log(den)


def _dot(a, b):
    return jnp.dot(a, b, preferred_element_type=F32)


def _dot_t(a, b):
    return lax.dot_general(a, b, (((1,), (1,)), ((), ())), preferred_element_type=F32)


def _rope_tables(pos):
    half = HEAD_DIM // 2
    inv = ROPE_THETA ** (-jnp.arange(half, dtype=F32) / half)
    ang = pos.astype(F32)[:, None] * inv[None, :]
    cos, sin = jnp.cos(ang), jnp.sin(ang)
    return jnp.concatenate([cos, cos], axis=-1), jnp.concatenate([-sin, sin], axis=-1)


def _rope_head(x, c, s):
    return x * c + pltpu.roll(x, HEAD_DIM // 2, axis=x.ndim - 1) * s


def _norm_mod_kernel(x_ref, g_ref, sh_ref, sc_ref, o_ref):
    x = x_ref[...]
    y = x * lax.rsqrt(jnp.mean(x * x, axis=-1, keepdims=True) + NORM_EPS) * g_ref[...]
    y = y * (1.0 + sc_ref[...]) + sh_ref[...]
    o_ref[...] = y.reshape(o_ref.shape).astype(o_ref.dtype)


def _seq_tiles(n_seq, seq_len, rows):
    if seq_len >= rows:
        return 1, rows
    return min(n_seq, rows // seq_len), seq_len


def norm_mod(h, g, mod, shift_col, scale_col, rows=256):
    n_seq, L, D = h.shape
    bs, tl = _seq_tiles(n_seq, L, rows)
    nl = L // tl
    return pl.pallas_call(
        _norm_mod_kernel,
        grid=(n_seq // bs, nl),
        in_specs=[
            pl.BlockSpec((bs, tl, D), lambda b, l: (b, l, 0)),
            pl.BlockSpec((1, D), lambda b, l: (0, 0)),
            pl.BlockSpec((bs, 1, D), lambda b, l: (b, 0, shift_col)),
            pl.BlockSpec((bs, 1, D), lambda b, l: (b, 0, scale_col)),
        ],
        out_specs=pl.BlockSpec((bs * tl, D), lambda b, l: (b * nl + l, 0)),
        out_shape=jax.ShapeDtypeStruct((n_seq * L, D), BF16),
        compiler_params=_cparams(2),
        name="norm_mod",
    )(h, g.reshape(1, D), mod, mod)


def fused_matmul(x, weights, w_specs, extras, extra_specs, out_shapes, out_specs, epilogue, *,
                 grid, x_spec, w_block, prologue=None, prefetch=(), recast=None, name="matmul"):
    nw, ne, no, npf = len(weights), len(extras), len(out_shapes), len(prefetch)

    def body(*refs):
        pf = refs[:npf]
        x_ref = refs[npf]
        w_refs = refs[npf + 1:npf + 1 + nw]
        e_refs = refs[npf + 1 + nw:npf + 1 + nw + ne]
        o_refs = refs[npf + 1 + nw + ne:npf + 1 + nw + ne + no]
        wb = refs[npf + 1 + nw + ne + no:]
        j, i = pl.program_id(0), pl.program_id(1)
        cond = (i == 0) if recast is None else recast(j, i, *pf)

        @pl.when(cond)
        def _():
            for k in range(nw):
                wb[k][...] = w_refs[k][...].astype(BF16)

        xv = x_ref[...]
        if prologue is not None:
            xv = prologue(xv)
        accs = [_dot(xv, wb[k][...]) for k in range(nw)]
        epilogue(accs, e_refs, o_refs, j, i, pf)

    gs = pltpu.PrefetchScalarGridSpec(
        num_scalar_prefetch=npf,
        grid=grid,
        in_specs=[x_spec] + list(w_specs) + list(extra_specs),
        out_specs=list(out_specs),
        scratch_shapes=[pltpu.VMEM(w_block, BF16) for _ in range(nw)],
    )
    return pl.pallas_call(body, grid_spec=gs, out_shape=list(out_shapes),
                          compiler_params=_cparams(2), name=name)(*prefetch, x, *weights, *extras)


def mod_proj(c, w, b, tn=1024):
    n, K = c.shape
    N = w.shape[-1]

    def epi(accs, e_refs, o_refs, j, i, pf):
        o_refs[0][...] = accs[0] + e_refs[0][...]

    return fused_matmul(
        c, [w], [pl.BlockSpec((K, tn), lambda j, i: (0, j))],
        [b.reshape(1, N)], [pl.BlockSpec((1, tn), lambda j, i: (0, j))],
        [jax.ShapeDtypeStruct((n, N), F32)], [pl.BlockSpec((n, tn), lambda j, i: (0, j))],
        epi, grid=(N // tn, 1), x_spec=pl.BlockSpec((n, K), lambda j, i: (0, 0)),
        w_block=(K, tn), prologue=lambda v: _silu(v).astype(BF16), name="mod_proj")[0]


def proj_rope(xn, w, n_seq, L, cos, sin, rope_pred, *, tn, rows=512, n_cols=None, sigmoid_out=False,
              name="proj"):
    K = xn.shape[1]
    N = w.shape[-1] if n_cols is None else n_cols
    bs, tl = _seq_tiles(n_seq, L, rows)
    nl = L // tl
    tm = bs * tl
    n_heads = tn // HEAD_DIM

    def epi(accs, e_refs, o_refs, j, i, pf):
        acc = accs[0]
        if rope_pred is None:
            if sigmoid_out:
                acc = jax.nn.sigmoid(acc)
            o_refs[0][...] = acc.reshape(bs, tl, tn)
            return
        do_rope = rope_pred(j)

        @pl.when(do_rope)
        def _():
            c = e_refs[0][...][None]
            s = e_refs[1][...][None]
            for h in range(n_heads):
                hs = slice(h * HEAD_DIM, (h + 1) * HEAD_DIM)
                a = acc[:, hs]
                r = pltpu.roll(a, HEAD_DIM // 2, axis=1)
                o_refs[0][:, :, hs] = a.reshape(bs, tl, HEAD_DIM) * c + r.reshape(bs, tl, HEAD_DIM) * s

        @pl.when(jnp.logical_not(do_rope))
        def _():
            o_refs[0][...] = acc.reshape(bs, tl, tn)

    w_spec = pl.BlockSpec((K, tn), lambda j, i: (0, j))
    tab_spec = pl.BlockSpec((tl, HEAD_DIM), lambda j, i: (i % nl, 0))
    return fused_matmul(
        xn, [w], [w_spec], [cos, sin], [tab_spec, tab_spec],
        [jax.ShapeDtypeStruct((n_seq, L, N), F32)],
        [pl.BlockSpec((bs, tl, tn), lambda j, i: (i // nl, i % nl, j))],
        epi, grid=(N // tn, (n_seq * L) // tm), x_spec=pl.BlockSpec((tm, K), lambda j, i: (i, 0)),
        w_block=(K, tn), name=name)[0]


def proj_residual(x, w, h, mod, gate_col, *, tn=512, rows=512, name="proj_res"):
    n_seq, L, N = h.shape
    K = x.shape[1]
    bs, tl = _seq_tiles(n_seq, L, rows)
    nl = L // tl
    tm = bs * tl
    ncb = N // tn

    def epi(accs, e_refs, o_refs, j, i, pf):
        o_refs[0][...] = e_refs[0][...] + e_refs[1][...] * accs[0].reshape(bs, tl, tn)

    blk = pl.BlockSpec((bs, tl, tn), lambda j, i: (i // nl, i % nl, j))
    return fused_matmul(
        x, [w], [pl.BlockSpec((K, tn), lambda j, i: (0, j))],
        [h, mod], [blk, pl.BlockSpec((bs, 1, tn), lambda j, i: (i // nl, 0, gate_col * ncb + j))],
        [jax.ShapeDtypeStruct((n_seq, L, N), F32)], [blk],
        epi, grid=(ncb, (n_seq * L) // tm), x_spec=pl.BlockSpec((tm, K), lambda j, i: (i, 0)),
        w_block=(K, tn), name=name)[0]


def swiglu_in(xn, w_in, d_ff, *, tn=512, tm=512, name="swiglu_in"):
    M, K = xn.shape
    ncb = d_ff // tn

    def epi(accs, e_refs, o_refs, j, i, pf):
        o_refs[0][...] = (_silu(accs[0]) * accs[1]).astype(BF16)

    return fused_matmul(
        xn, [w_in, w_in],
        [pl.BlockSpec((K, tn), lambda j, i: (0, j)), pl.BlockSpec((K, tn), lambda j, i: (0, ncb + j))],
        [], [], [jax.ShapeDtypeStruct((M, d_ff), BF16)], [pl.BlockSpec((tm, tn), lambda j, i: (i, j))],
        epi, grid=(ncb, M // tm), x_spec=pl.BlockSpec((tm, K), lambda j, i: (i, 0)),
        w_block=(K, tn), name=name)[0]


def _band_kernel(q_ref, kc_ref, kp_ref, vc_ref, vp_ref, o_ref, lse_ref, *, n_back):
    i = pl.program_id(2)
    qb = q_ref.shape[0]
    qi = lax.broadcasted_iota(jnp.int32, (qb, 2 * qb), 0)
    kj = lax.broadcasted_iota(jnp.int32, (qb, 2 * qb), 1)
    rel = qi + qb - kj
    mask = (rel >= 0) & (rel <= n_back) & ((kj >= qb) | (i > 0))
    lane = lax.broadcasted_iota(jnp.int32, (qb, A_HEADS), 1)
    lse_all = jnp.zeros((qb, A_HEADS), F32)
    for h in range(A_HEADS):
        hs = slice(h * HEAD_DIM, (h + 1) * HEAD_DIM)
        q = q_ref[:, hs].astype(BF16)
        k = jnp.concatenate([kp_ref[:, hs], kc_ref[:, hs]], axis=0).astype(BF16)
        v = jnp.concatenate([vp_ref[:, hs], vc_ref[:, hs]], axis=0).astype(BF16)
        s = _dot_t(q, k) * ATT_SCALE
        p, lse = _masked_softmax(s, mask)
        o_ref[:, hs] = _dot(p.astype(BF16), v)
        lse_all = jnp.where(lane == h, lse, lse_all)
    lse_ref[...] = lse_all


def band_attention_a(qkv, g, dil):
    B, L, NC = qkv.shape
    n_back = A_GROUPS[g][0] // dil
    assert n_back == Q_BLOCK
    Lr = L // dil
    qb = math.gcd(Lr, Q_BLOCK)
    nb = Lr // qb
    ncb = NC // A_COLS
    view = qkv.reshape(B, Lr, dil * NC)

    def col(part):
        return lambda b, r, i: (b, i, r * ncb + g * 3 + part)

    def col_prev(part):
        return lambda b, r, i: (b, jnp.maximum(i - 1, 0), r * ncb + g * 3 + part)

    blk = (None, qb, A_COLS)
    o, lse = pl.pallas_call(
        functools.partial(_band_kernel, n_back=n_back),
        grid=(B, dil, nb),
        in_specs=[pl.BlockSpec(blk, col(0)), pl.BlockSpec(blk, col(1)), pl.BlockSpec(blk, col_prev(1)),
                  pl.BlockSpec(blk, col(2)), pl.BlockSpec(blk, col_prev(2))],
        out_specs=[pl.BlockSpec(blk, lambda b, r, i: (b, i, r)),
                   pl.BlockSpec((None, None, qb, A_HEADS), lambda b, r, i: (b, r, i, 0))],
        out_shape=[jax.ShapeDtypeStruct((B, Lr, dil * A_COLS), F32),
                   jax.ShapeDtypeStruct((B, dil, Lr, A_HEADS), F32)],
        compiler_params=_cparams(3),
        name="band_attention_a",
    )(view, view, view, view, view)
    return o.reshape(B, L, A_COLS), lse.transpose(0, 2, 1, 3).reshape(B, L, A_HEADS)


def _sample_a_kernel(q_ref, kn_ref, vn_ref, kb_ref, vb_ref, o_ref, lse_ref, *, dil, heads):
    nq = q_ref.shape[0]
    W = kb_ref.shape[0]
    qi = lax.broadcasted_iota(jnp.int32, (nq, W), 0)
    m = lax.broadcasted_iota(jnp.int32, (nq, W), 1)
    mask_b = (m >= qi) & (((m - qi) % dil) == 0)
    qn = lax.broadcasted_iota(jnp.int32, (nq, nq), 0)
    nn = lax.broadcasted_iota(jnp.int32, (nq, nq), 1)
    mask_n = (nn <= qn) & (((qn - nn) % dil) == 0)
    lane = lax.broadcasted_iota(jnp.int32, (nq, heads), 1)
    lse_all = jnp.zeros((nq, heads), F32)
    for h in range(heads):
        hs = slice(h * HEAD_DIM, (h + 1) * HEAD_DIM)
        q = q_ref[:, hs].astype(BF16)
        sb = jnp.where(mask_b, _dot_t(q, kb_ref[:, hs].astype(BF16)) * ATT_SCALE, NEG_INF)
        sn = jnp.where(mask_n, _dot_t(q, kn_ref[:, hs].astype(BF16)) * ATT_SCALE, NEG_INF)
        mx = jnp.maximum(jnp.max(sb, axis=-1, keepdims=True), jnp.max(sn, axis=-1, keepdims=True))
        eb = jnp.where(mask_b, jnp.exp(sb - mx), 0.0)
        en = jnp.where(mask_n, jnp.exp(sn - mx), 0.0)
        den = jnp.maximum(jnp.sum(eb, axis=-1, keepdims=True) + jnp.sum(en, axis=-1, keepdims=True), 1e-30)
        o = _dot((eb / den).astype(BF16), vb_ref[:, hs].astype(BF16))
        o = o + _dot((en / den).astype(BF16), vn_ref[:, hs].astype(BF16))
        o_ref[:, hs] = o
        lse_all = jnp.where(lane == h, mx + jnp.log(den), lse_all)
    lse_ref[...] = lse_all


def sample_attention_a(qkv, buf, g, dil, heads=4):
    S, Q, NC = qkv.shape
    W = buf.shape[1]
    assert W == A_GROUPS[g][0] and W % dil == 0 and W // dil == Q_BLOCK
    nhb = A_HEADS // heads
    hc = heads * HEAD_DIM
    per = A_COLS // hc

    def new(part):
        return pl.BlockSpec((None, Q, hc), lambda s, hb: (s, 0, (g * 3 + part) * per + hb))

    o, lse = pl.pallas_call(
        functools.partial(_sample_a_kernel, dil=dil, heads=heads),
        grid=(S, nhb),
        in_specs=[new(0), new(1), new(2),
                  pl.BlockSpec((None, W, hc), lambda s, hb: (s, 0, hb)),
                  pl.BlockSpec((None, W, hc), lambda s, hb: (s, 0, per + hb))],
        out_specs=[pl.BlockSpec((None, Q, hc), lambda s, hb: (s, 0, hb)),
                   pl.BlockSpec((None, None, Q, heads), lambda s, hb: (s, hb, 0, 0))],
        out_shape=[jax.ShapeDtypeStruct((S, Q, A_COLS), F32),
                   jax.ShapeDtypeStruct((S, nhb, Q, heads), F32)],
        compiler_params=_cparams(2),
        name="sample_attention_a",
    )(qkv, qkv, qkv, buf, buf)
    return o, lse.transpose(0, 2, 1, 3).reshape(S, Q, A_HEADS)


def _merge_wo_kernel(o0_ref, o1_ref, o2_ref, l0_ref, l1_ref, l2_ref, w_ref, h_ref, gt_ref, out_ref, wb):
    @pl.when(pl.program_id(0) == 0)
    def _():
        wb[...] = w_ref[...].astype(BF16)

    bs, tl, N = h_ref.shape
    ls = [l0_ref[...], l1_ref[...], l2_ref[...]]
    mx = jnp.maximum(jnp.maximum(ls[0], ls[1]), ls[2])
    es = [jnp.exp(l - mx) for l in ls]
    den = es[0] + es[1] + es[2]
    ws = [e / den for e in es]
    o_refs = (o0_ref, o1_ref, o2_ref)
    parts = []
    for h in range(A_HEADS):
        hs = slice(h * HEAD_DIM, (h + 1) * HEAD_DIM)
        acc = ws[0][:, h:h + 1] * o_refs[0][:, hs]
        for g in (1, 2):
            acc = acc + ws[g][:, h:h + 1] * o_refs[g][:, hs]
        parts.append(acc.astype(BF16))
    merged = jnp.concatenate(parts, axis=1)
    y = _dot(merged, wb[...])
    out_ref[...] = h_ref[...] + gt_ref[...] * y.reshape(bs, tl, N)


def merge_wo(os_, lses, w_o, h, mod, gate_col, rows=256):
    n_seq, L, N = h.shape
    bs, tl = _seq_tiles(n_seq, L, rows)
    nl = L // tl
    tm = bs * tl
    K = w_o.shape[0]
    o_spec = pl.BlockSpec((tm, K), lambda i: (i, 0))
    l_spec = pl.BlockSpec((tm, A_HEADS), lambda i: (i, 0))
    blk = pl.BlockSpec((bs, tl, N), lambda i: (i // nl, i % nl, 0))
    return pl.pallas_call(
        _merge_wo_kernel,
        grid=((n_seq * L) // tm,),
        in_specs=[o_spec] * 3 + [l_spec] * 3 + [
            pl.BlockSpec((K, N), lambda i: (0, 0)), blk,
            pl.BlockSpec((bs, 1, N), lambda i: (i // nl, 0, gate_col))],
        out_specs=blk,
        out_shape=jax.ShapeDtypeStruct((n_seq, L, N), F32),
        scratch_shapes=[pltpu.VMEM((K, N), BF16)],
        compiler_params=_cparams(1),
        name="merge_wo",
    )(*[o.reshape(n_seq * L, K) for o in os_], *[l.reshape(n_seq * L, A_HEADS) for l in lses], w_o, h, mod)


PAGES_PER_SEQ = 16
CHUNKS_PER_PAGE = PAGE_SIZE // CMP_STRIDE
N_CHUNK = PAGES_PER_SEQ * CHUNKS_PER_PAGE
CMP_PLANE = B_KV_HEADS * HEAD_DIM
HALF_FLAT = CMP_STRIDE * HEAD_DIM


def _compress_kernel(pt_ref, *refs):
    pages = refs[:PAGES_PER_SEQ]
    (w1k_ref, w2k_ref, pek_ref, w1v_ref, w2v_ref, pev_ref, cos_ref, sin_ref,
     kc_ref, vc_ref, w1k_b, w1v_b, w2k_b, w2v_b, pe_hid, stage) = refs[PAGES_PER_SEQ:]

    @pl.when(pl.program_id(0) == 0)
    def _():
        w1k_b[...] = w1k_ref[...].astype(BF16)
        w1v_b[...] = w1v_ref[...].astype(BF16)
        w2k_b[...] = w2k_ref[...].astype(BF16)
        w2v_b[...] = w2v_ref[...].astype(BF16)
        for plane, (pe_ref, w1_b) in enumerate(((pek_ref, w1k_b), (pev_ref, w1v_b))):
            flat = jnp.concatenate([pe_ref[r:r + 1, :] for r in range(CMP_LEN)], axis=1)
            flat = jnp.broadcast_to(flat, (SUBLANES, CMP_LEN * HEAD_DIM)).astype(BF16)
            pe_hid[plane] = _dot(flat, w1_b[...])

    for p in range(PAGES_PER_SEQ):
        for c in range(2 * B_KV_HEADS):
            stage[c, p * PAGE_SIZE:(p + 1) * PAGE_SIZE, :] = pages[p][:, c * HEAD_DIM:(c + 1) * HEAD_DIM]

    for plane, (w1_b, w2_b, out_ref) in enumerate(((w1k_b, w2k_b, kc_ref), (w1v_b, w2v_b, vc_ref))):
        lhs = jnp.concatenate(
            [jnp.concatenate([stage[plane * B_KV_HEADS + h, pl.ds(r, N_CHUNK, stride=CMP_STRIDE), :].astype(BF16)
                              for r in range(CMP_STRIDE)], axis=1)
             for h in range(B_KV_HEADS)], axis=0)
        first = _dot(lhs, w1_b[:HALF_FLAT, :])
        second = _dot(lhs, w1_b[HALF_FLAT:, :])
        hid = first + pltpu.roll(second, B_KV_HEADS * N_CHUNK - 1, axis=0) + pe_hid[plane][0:1, :]
        out = _dot(_silu(hid).astype(BF16), w2_b[...])
        for h in range(B_KV_HEADS):
            blk = out[h * N_CHUNK:(h + 1) * N_CHUNK, :]
            if plane == 0:
                blk = _rope_head(blk, cos_ref[...], sin_ref[...])
            out_ref[:, h * HEAD_DIM:(h + 1) * HEAD_DIM] = blk


def compress_pages(pages, page_table, w):
    n_seq = page_table.shape[0]
    cos, sin = _rope_tables(jnp.arange(N_CHUNK, dtype=jnp.int32) * CMP_STRIDE + CMP_LEN - 1)

    def page_spec(p):
        return pl.BlockSpec((None, PAGE_SIZE, 2 * CMP_PLANE), lambda s, pt: (pt[s, p], 0, 0))

    def const(shape):
        return pl.BlockSpec(shape, lambda s, pt: (0,) * len(shape))

    flat_rows = CMP_LEN * HEAD_DIM
    gs = pltpu.PrefetchScalarGridSpec(
        num_scalar_prefetch=1,
        grid=(n_seq,),
        in_specs=[page_spec(p) for p in range(PAGES_PER_SEQ)] + [
            const((flat_rows, CMP_HIDDEN)), const((CMP_HIDDEN, HEAD_DIM)), const((CMP_LEN, HEAD_DIM)),
            const((flat_rows, CMP_HIDDEN)), const((CMP_HIDDEN, HEAD_DIM)), const((CMP_LEN, HEAD_DIM)),
            const((N_CHUNK, HEAD_DIM)), const((N_CHUNK, HEAD_DIM))],
        out_specs=[pl.BlockSpec((None, N_CHUNK, CMP_PLANE), lambda s, pt: (s, 0, 0))] * 2,
        scratch_shapes=[pltpu.VMEM((flat_rows, CMP_HIDDEN), BF16), pltpu.VMEM((flat_rows, CMP_HIDDEN), BF16),
                        pltpu.VMEM((CMP_HIDDEN, HEAD_DIM), BF16), pltpu.VMEM((CMP_HIDDEN, HEAD_DIM), BF16),
                        pltpu.VMEM((2, SUBLANES, CMP_HIDDEN), F32),
                        pltpu.VMEM((2 * B_KV_HEADS, PAGES_PER_SEQ * PAGE_SIZE, HEAD_DIM), F32)],
    )
    return pl.pallas_call(
        _compress_kernel, grid_spec=gs,
        out_shape=[jax.ShapeDtypeStruct((n_seq, N_CHUNK, CMP_PLANE), F32)] * 2,
        compiler_params=_cparams(1), name="compress_pages",
    )(page_table, *([pages] * PAGES_PER_SEQ), w['cmp_w1_k'], w['cmp_w2_k'], w['cmp_pe_k'],
      w['cmp_w1_v'], w['cmp_w2_v'], w['cmp_pe_v'], cos, sin)


def _overlap_matrix(n_slc):
    c0 = np.arange(LANES) * CMP_STRIDE
    s0 = np.arange(LANES) * SLC_LEN
    m = (c0[:, None] < s0[None, :] + SLC_LEN) & (c0[:, None] + CMP_LEN > s0[None, :])
    m &= (np.arange(LANES)[:, None] < N_CHUNK - 1) & (np.arange(LANES)[None, :] < n_slc)
    return jnp.asarray(m.astype(np.float32), dtype=BF16)


def _expand_matrix(n_keys):
    m = (np.arange(n_keys)[None, :] // SLC_LEN) == np.arange(LANES)[:, None]
    return jnp.asarray(m.astype(np.float32), dtype=BF16)


def _select_blocks(p_slc, q_pos, n_slc):
    nq = p_slc.shape[0]
    blk = lax.broadcasted_iota(jnp.int32, (nq, LANES), 1)
    cur = q_pos // SLC_LEN
    forced = (blk == 0) | (blk == cur) | (blk == cur - 1)
    imp = jnp.where(forced, FORCE_SCORE, p_slc)
    imp = jnp.where(blk <= cur, imp, NEG_INF)
    imp = jnp.where(blk < n_slc, imp, -3e38)
    rank = jnp.zeros((nq, LANES), jnp.int32)
    for k in range(n_slc):
        vk = imp[:, k:k + 1]
        ahead = (vk > imp) | ((vk == imp) & (blk > k))
        rank = rank + ahead.astype(jnp.int32)
    sel = (rank < min(N_SELECT, n_slc)) & (blk < n_slc)
    return jnp.where(sel, 1.0, 0.0).astype(BF16)


def _cmp_branch(q, kc, vc, q_pos, overlap, n_q):
    s = _dot_t(q, kc.astype(BF16)) * ATT_SCALE
    n = lax.broadcasted_iota(jnp.int32, (n_q, N_CHUNK), 1)
    mask = (n * CMP_STRIDE + CMP_LEN - 1 <= q_pos) & (n < N_CHUNK - 1)
    p, _ = _masked_softmax(s.reshape(B_HPG, n_q, N_CHUNK), mask[None])
    o = _dot(p.reshape(B_HPG * n_q, N_CHUNK).astype(BF16), vc.astype(BF16))
    p_grp = jnp.sum(p, axis=0)
    return o, _dot(p_grp.astype(BF16), overlap)


def _nsa_prompt_kernel(q_ref, g_ref, kc_ref, vc_ref, ks_ref, vs_ref, kw_ref, vw_ref, ov_ref, ex_ref,
                       o_ref, ks_b, vs_b, kw_b, vw_b):
    i = pl.program_id(2)
    L = ks_ref.shape[0]
    nq = q_ref.shape[0]
    n_slc = L // SLC_LEN

    @pl.when(i == 0)
    def _():
        ks_b[...] = ks_ref[...].astype(BF16)
        vs_b[...] = vs_ref[...].astype(BF16)
        kw_b[...] = kw_ref[...].astype(BF16)
        vw_b[...] = vw_ref[...].astype(BF16)

    q = jnp.concatenate([q_ref[:, j * HEAD_DIM:(j + 1) * HEAD_DIM] for j in range(B_HPG)], axis=0).astype(BF16)
    q_pos = i * nq + lax.broadcasted_iota(jnp.int32, (nq, 1), 0)

    o_cmp, p_slc = _cmp_branch(q, kc_ref[...], vc_ref[...], q_pos, ov_ref[...], nq)
    sel = _select_blocks(p_slc, q_pos, n_slc)
    sel_keys = _dot(sel, ex_ref[...])

    k_pos = lax.broadcasted_iota(jnp.int32, (nq, L), 1)
    mask = (sel_keys > 0.5) & (k_pos <= q_pos)
    s = _dot_t(q, ks_b[...]) * ATT_SCALE
    p, _ = _masked_softmax(s.reshape(B_HPG, nq, L), mask[None])
    o_slc = _dot(p.reshape(B_HPG * nq, L).astype(BF16), vs_b[...])

    n_win = B_WINDOW + nq
    start = pl.multiple_of(jnp.maximum(i * nq - B_WINDOW, 0), nq)
    kw = kw_b[pl.ds(start, n_win), :]
    vw = vw_b[pl.ds(start, n_win), :]
    rel = q_pos - (start + lax.broadcasted_iota(jnp.int32, (nq, n_win), 1))
    mask_w = (rel >= 0) & (rel <= B_WINDOW)
    s = _dot_t(q, kw) * ATT_SCALE
    p, _ = _masked_softmax(s.reshape(B_HPG, nq, n_win), mask_w[None])
    o_win = _dot(p.reshape(B_HPG * nq, n_win).astype(BF16), vw)

    gates = g_ref[...]
    for j in range(B_HPG):
        rows = slice(j * nq, (j + 1) * nq)
        c = j * N_BRANCH
        o_ref[:, j * HEAD_DIM:(j + 1) * HEAD_DIM] = (
            gates[:, c:c + 1] * o_cmp[rows] + gates[:, c + 1:c + 2] * o_slc[rows] + gates[:, c + 2:c + 3] * o_win[rows])


def nsa_prompt(q, gates, kc, vc, rows, win):
    B, L, _ = q.shape
    nq = Q_BLOCK
    assert L == N_CHUNK * CMP_STRIDE and L % SLC_LEN == 0
    gw = B_HPG * HEAD_DIM
    full = (None, L, HEAD_DIM)
    return pl.pallas_call(
        _nsa_prompt_kernel,
        grid=(B, B_KV_HEADS, L // nq),
        in_specs=[
            pl.BlockSpec((None, nq, gw), lambda b, g, i: (b, i, g)),
            pl.BlockSpec((None, None, nq, B_HPG * N_BRANCH), lambda b, g, i: (b, g, i, 0)),
            pl.BlockSpec((None, N_CHUNK, HEAD_DIM), lambda b, g, i: (b, 0, g)),
            pl.BlockSpec((None, N_CHUNK, HEAD_DIM), lambda b, g, i: (b, 0, g)),
            pl.BlockSpec(full, lambda b, g, i: (b, 0, 2 * B_KV_HEADS + g)),
            pl.BlockSpec(full, lambda b, g, i: (b, 0, 3 * B_KV_HEADS + g)),
            pl.BlockSpec(full, lambda b, g, i: (b, 0, g)),
            pl.BlockSpec(full, lambda b, g, i: (b, 0, B_KV_HEADS + g)),
            pl.BlockSpec((LANES, LANES), lambda b, g, i: (0, 0)),
            pl.BlockSpec((LANES, L), lambda b, g, i: (0, 0)),
        ],
        out_specs=pl.BlockSpec((None, nq, gw), lambda b, g, i: (b, i, g)),
        out_shape=jax.ShapeDtypeStruct((B, L, B_HEADS * HEAD_DIM), F32),
        scratch_shapes=[pltpu.VMEM((L, HEAD_DIM), BF16) for _ in range(4)],
        compiler_params=_cparams(3),
        name="nsa_prompt",
    )(q, gates, kc, vc, rows, rows, win, win, _overlap_matrix(L // SLC_LEN), _expand_matrix(L))


def _nsa_sample_kernel(pt_ref, *refs, past_len):
    pages = refs[:PAGES_PER_SEQ]
    (q_ref, g_ref, kc_ref, vc_ref, new_ref, wbuf_ref, wnew_ref, ov_ref, ex_ref, o_ref) = refs[PAGES_PER_SEQ:]
    nq = q_ref.shape[0]
    n_slc = -(-(past_len + nq) // SLC_LEN)
    q_pos = past_len + lax.broadcasted_iota(jnp.int32, (nq, 1), 0)
    qn = lax.broadcasted_iota(jnp.int32, (nq, nq), 0)
    nn = lax.broadcasted_iota(jnp.int32, (nq, nq), 1)
    causal_new = nn <= qn
    n_buf = wbuf_ref.shape[0]
    mb = lax.broadcasted_iota(jnp.int32, (nq, n_buf), 1)
    rel_b = q_pos - (past_len - n_buf + mb)
    mask_wb = (rel_b >= 0) & (rel_b <= B_WINDOW)
    rel_n = qn - nn
    mask_wn = (rel_n >= 0) & (rel_n <= B_WINDOW)
    gates = g_ref[...]

    for g in range(B_KV_HEADS):
        gs = slice(g * HEAD_DIM, (g + 1) * HEAD_DIM)
        vs_ = slice(CMP_PLANE + g * HEAD_DIM, CMP_PLANE + (g + 1) * HEAD_DIM)
        q = jnp.concatenate(
            [q_ref[:, (g * B_HPG + j) * HEAD_DIM:(g * B_HPG + j + 1) * HEAD_DIM] for j in range(B_HPG)],
            axis=0).astype(BF16)
        o_cmp, p_slc = _cmp_branch(q, kc_ref[:, gs], vc_ref[:, gs], q_pos, ov_ref[...], nq)
        sel = _select_blocks(p_slc, q_pos, n_slc)
        sel_keys = _dot(sel, ex_ref[...])

        k_past = jnp.concatenate([pg[:, gs] for pg in pages], axis=0).astype(BF16)
        v_past = jnp.concatenate([pg[:, vs_] for pg in pages], axis=0).astype(BF16)
        k_new = new_ref[:, 2 * CMP_PLANE + g * HEAD_DIM:2 * CMP_PLANE + (g + 1) * HEAD_DIM].astype(BF16)
        v_new = new_ref[:, 3 * CMP_PLANE + g * HEAD_DIM:3 * CMP_PLANE + (g + 1) * HEAD_DIM].astype(BF16)
        mask_p = sel_keys[:, :past_len] > 0.5
        mask_n = (sel_keys[:, past_len:past_len + nq] > 0.5) & causal_new
        o_slc = _two_part_attention(q, k_past, v_past, mask_p, k_new, v_new, mask_n, nq)

        kw = wbuf_ref[:, gs].astype(BF16)
        vw = wbuf_ref[:, vs_].astype(BF16)
        kwn = wnew_ref[:, gs].astype(BF16)
        vwn = wnew_ref[:, vs_].astype(BF16)
        o_win = _two_part_attention(q, kw, vw, mask_wb, kwn, vwn, mask_wn, nq)

        for j in range(B_HPG):
            rows = slice(j * nq, (j + 1) * nq)
            hd = g * B_HPG + j
            c = hd * N_BRANCH
            o_ref[:, hd * HEAD_DIM:(hd + 1) * HEAD_DIM] = (
                gates[:, c:c + 1] * o_cmp[rows] + gates[:, c + 1:c + 2] * o_slc[rows]
                + gates[:, c + 2:c + 3] * o_win[rows])


def _two_part_attention(q, k1, v1, mask1, k2, v2, mask2, nq):
    n1, n2 = k1.shape[0], k2.shape[0]
    s1 = jnp.where(mask1[None], (_dot_t(q, k1) * ATT_SCALE).reshape(B_HPG, nq, n1), NEG_INF)
    s2 = jnp.where(mask2[None], (_dot_t(q, k2) * ATT_SCALE).reshape(B_HPG, nq, n2), NEG_INF)
    m = jnp.maximum(jnp.max(s1, axis=-1, keepdims=True), jnp.max(s2, axis=-1, keepdims=True))
    e1 = jnp.where(mask1[None], jnp.exp(s1 - m), 0.0)
    e2 = jnp.where(mask2[None], jnp.exp(s2 - m), 0.0)
    den = jnp.maximum(jnp.sum(e1, axis=-1, keepdims=True) + jnp.sum(e2, axis=-1, keepdims=True), 1e-30)
    o = _dot((e1 / den).reshape(B_HPG * nq, n1).astype(BF16), v1)
    return o + _dot((e2 / den).reshape(B_HPG * nq, n2).astype(BF16), v2)


def nsa_sample(q, gates, kc, vc, pages, page_table, new_rows, win_buf, win_new, past_len):
    S, nq, _ = q.shape
    assert past_len == PAGES_PER_SEQ * PAGE_SIZE and nq <= SLC_LEN
    n_keys = past_len + LANES

    def page_spec(p):
        return pl.BlockSpec((None, PAGE_SIZE, 2 * CMP_PLANE), lambda s, pt: (pt[s, p], 0, 1))

    def per_seq(rows, cols):
        return pl.BlockSpec((None, rows, cols), lambda s, pt: (s, 0, 0))

    gs = pltpu.PrefetchScalarGridSpec(
        num_scalar_prefetch=1,
        grid=(S,),
        in_specs=[page_spec(p) for p in range(PAGES_PER_SEQ)] + [
            per_seq(nq, B_HEADS * HEAD_DIM), per_seq(nq, LANES), per_seq(N_CHUNK, CMP_PLANE),
            per_seq(N_CHUNK, CMP_PLANE), per_seq(nq, 4 * CMP_PLANE), per_seq(win_buf.shape[1], 2 * CMP_PLANE),
            per_seq(nq, 2 * CMP_PLANE),
            pl.BlockSpec((LANES, LANES), lambda s, pt: (0, 0)),
            pl.BlockSpec((LANES, n_keys), lambda s, pt: (0, 0))],
        out_specs=per_seq(nq, B_HEADS * HEAD_DIM),
    )
    n_slc = -(-(past_len + nq) // SLC_LEN)
    return pl.pallas_call(
        functools.partial(_nsa_sample_kernel, past_len=past_len), grid_spec=gs,
        out_shape=jax.ShapeDtypeStruct((S, nq, B_HEADS * HEAD_DIM), F32),
        compiler_params=_cparams(1), name="nsa_sample",
    )(page_table, *([pages] * PAGES_PER_SEQ), q, gates, kc, vc, new_rows, win_buf, win_new,
      _overlap_matrix(n_slc), _expand_matrix(n_keys))


MOE_TILE = 256


def _router_kernel(x_ref, w_ref, b_ref, gate_ref, sel_ref):
    logits = _dot(x_ref[...], w_ref[...].astype(BF16)) + b_ref[...]
    lane = lax.broadcasted_iota(jnp.int32, logits.shape, 1)
    logits = jnp.where(lane < N_EXPERTS, logits, -jnp.inf)
    m1 = jnp.max(logits, axis=-1, keepdims=True)
    i1 = jnp.min(jnp.where(logits == m1, lane, LANES), axis=-1, keepdims=True)
    rest = jnp.where(lane == i1, -jnp.inf, logits)
    m2 = jnp.max(rest, axis=-1, keepdims=True)
    i2 = jnp.min(jnp.where(rest == m2, lane, LANES), axis=-1, keepdims=True)
    e2 = jnp.exp(m2 - m1)
    den = 1.0 + e2
    gate_ref[...] = jnp.where(lane == i1, 1.0 / den, jnp.where(lane == i2, e2 / den, 0.0))
    sel_ref[...] = ((lane == i1) | (lane == i2)).astype(jnp.int32)


def moe_router(xn, w_router, b_router, tm=512):
    T, D = xn.shape
    wp = jnp.pad(w_router, ((0, 0), (0, LANES - N_EXPERTS)))
    bp = jnp.pad(b_router, (0, LANES - N_EXPERTS)).reshape(1, LANES)
    return pl.pallas_call(
        _router_kernel,
        grid=(T // tm,),
        in_specs=[pl.BlockSpec((tm, D), lambda i: (i, 0)), pl.BlockSpec((D, LANES), lambda i: (0, 0)),
                  pl.BlockSpec((1, LANES), lambda i: (0, 0))],
        out_specs=[pl.BlockSpec((tm, LANES), lambda i: (i, 0))] * 2,
        out_shape=[jax.ShapeDtypeStruct((T, LANES), F32), jax.ShapeDtypeStruct((T, LANES), jnp.int32)],
        compiler_params=_cparams(1), name="moe_router",
    )(xn, wp, bp)


def _row_copy(src_hbm, row, dst, r, sem):
    return pltpu.make_async_copy(src_hbm.at[pl.ds(row, 1), :], dst.at[pl.ds(r, 1), :], sem)


def _gather_rows_kernel(src_ref, x_hbm, o_ref, sem):
    tg = o_ref.shape[0]
    base = pl.program_id(0) * tg

    def start(r, c):
        _row_copy(x_hbm, src_ref[base + r], o_ref, r, sem).start()
        return c

    def wait(r, c):
        _row_copy(x_hbm, 0, o_ref, r, sem).wait()
        return c

    lax.fori_loop(0, tg, start, 0)
    lax.fori_loop(0, tg, wait, 0)


def gather_rows(x, src, tg=256):
    P = src.shape[0]
    C = x.shape[1]
    gs = pltpu.PrefetchScalarGridSpec(
        num_scalar_prefetch=1, grid=(P // tg,),
        in_specs=[pl.BlockSpec(memory_space=pl.ANY)],
        out_specs=pl.BlockSpec((tg, C), lambda i, src: (i, 0)),
        scratch_shapes=[pltpu.SemaphoreType.DMA(())],
    )
    return pl.pallas_call(_gather_rows_kernel, grid_spec=gs, out_shape=jax.ShapeDtypeStruct((P, C), x.dtype),
                          compiler_params=_cparams(1), name="moe_gather_rows")(src, x)


def moe_experts(x_sorted, w_in, w_out, tile_expert, tile_first, gate_sorted, tm=MOE_TILE):
    P, K = x_sorted.shape
    F = w_out.shape[1]
    tn1, tn2 = 1024, 512
    ncb = F // tn1

    def recast(j, i, te, tf):
        return tf[i] == 1

    def epi_in(accs, e_refs, o_refs, j, i, pf):
        o_refs[0][...] = (_silu(accs[0]) * accs[1]).astype(BF16)

    act = fused_matmul(
        x_sorted, [w_in, w_in],
        [pl.BlockSpec((None, K, tn1), lambda j, i, te, tf: (te[i], 0, j)),
         pl.BlockSpec((None, K, tn1), lambda j, i, te, tf: (te[i], 0, ncb + j))],
        [], [], [jax.ShapeDtypeStruct((P, F), BF16)],
        [pl.BlockSpec((tm, tn1), lambda j, i, te, tf: (i, j))],
        epi_in, grid=(ncb, P // tm), x_spec=pl.BlockSpec((tm, K), lambda j, i, te, tf: (i, 0)),
        w_block=(K, tn1), prefetch=(tile_expert, tile_first), recast=recast, name="moe_in")[0]

    def epi_out(accs, e_refs, o_refs, j, i, pf):
        o_refs[0][...] = accs[0] * e_refs[0][...]

    return fused_matmul(
        act, [w_out], [pl.BlockSpec((None, F, tn2), lambda j, i, te, tf: (te[i], 0, j))],
        [gate_sorted.reshape(P, 1)], [pl.BlockSpec((tm, 1), lambda j, i, te, tf: (i, 0))],
        [jax.ShapeDtypeStruct((P, K), F32)], [pl.BlockSpec((tm, tn2), lambda j, i, te, tf: (i, j))],
        epi_out, grid=(K // tn2, P // tm), x_spec=pl.BlockSpec((tm, F), lambda j, i, te, tf: (i, 0)),
        w_block=(F, tn2), prefetch=(tile_expert, tile_first), recast=recast, name="moe_out")[0]


def _combine_kernel(pa_ref, pb_ref, eo_hbm, h_ref, gt_ref, g_ref, out_ref, buf_a, buf_b, sem, *, tok_offset):
    bs, tl, D = h_ref.shape
    tc = bs * tl
    base = tok_offset + pl.program_id(0) * tc

    def start(r, c):
        _row_copy(eo_hbm, pa_ref[base + r], buf_a, r, sem.at[0]).start()
        _row_copy(eo_hbm, pb_ref[base + r], buf_b, r, sem.at[1]).start()
        return c

    def wait(r, c):
        _row_copy(eo_hbm, 0, buf_a, r, sem.at[0]).wait()
        _row_copy(eo_hbm, 0, buf_b, r, sem.at[1]).wait()
        return c

    lax.fori_loop(0, tc, start, 0)
    lax.fori_loop(0, tc, wait, 0)
    y = buf_a[...] + buf_b[...]
    hn = h_ref[...] + gt_ref[...] * y.reshape(bs, tl, D)
    out_ref[...] = hn * lax.rsqrt(jnp.mean(hn * hn, axis=-1, keepdims=True) + NORM_EPS) * g_ref[...]


def moe_combine_norm(eo, pos_a, pos_b, h, mod, gate_col, g, tok_offset, rows=128):
    n_seq, L, D = h.shape
    bs, tl = _seq_tiles(n_seq, L, rows)
    nl = L // tl
    blk = pl.BlockSpec((bs, tl, D), lambda i, pa, pb: (i // nl, i % nl, 0))
    gs = pltpu.PrefetchScalarGridSpec(
        num_scalar_prefetch=2, grid=((n_seq * L) // (bs * tl),),
        in_specs=[pl.BlockSpec(memory_space=pl.ANY), blk,
                  pl.BlockSpec((bs, 1, D), lambda i, pa, pb: (i // nl, 0, gate_col)),
                  pl.BlockSpec((1, D), lambda i, pa, pb: (0, 0))],
        out_specs=blk,
        scratch_shapes=[pltpu.VMEM((bs * tl, D), F32), pltpu.VMEM((bs * tl, D), F32),
                        pltpu.SemaphoreType.DMA((2,))],
    )
    return pl.pallas_call(
        functools.partial(_combine_kernel, tok_offset=tok_offset), grid_spec=gs,
        out_shape=jax.ShapeDtypeStruct((n_seq, L, D), F32),
        compiler_params=_cparams(1), name="moe_combine_norm",
    )(pos_a, pos_b, eo, h, mod, g.reshape(1, D))


def _moe_plan(gate, sel, tm):
    T = gate.shape[0]
    P = TOP_K * T + N_EXPERTS * tm
    sel8 = sel[:, :N_EXPERTS]
    cnt = jnp.sum(sel8, axis=0)
    rank = jnp.cumsum(sel8, axis=0) - sel8
    cnt_pad = ((cnt + tm - 1) // tm) * tm
    ends = jnp.cumsum(cnt_pad)
    pos = (ends - cnt_pad)[None, :] + rank
    chosen = sel8 > 0
    pos_drop = jnp.where(chosen, pos, P).reshape(-1)
    tok = jnp.broadcast_to(jnp.arange(T, dtype=jnp.int32)[:, None], (T, N_EXPERTS)).reshape(-1)
    src = jnp.zeros((P,), jnp.int32).at[pos_drop].set(tok, mode='drop')
    gate_sorted = jnp.zeros((P,), F32).at[pos_drop].set(gate[:, :N_EXPERTS].reshape(-1), mode='drop')
    pos_a = jnp.min(jnp.where(chosen, pos, P), axis=1).astype(jnp.int32)
    pos_b = jnp.max(jnp.where(chosen, pos, -1), axis=1).astype(jnp.int32)
    n_tiles = P // tm
    tile_start = jnp.arange(n_tiles, dtype=jnp.int32) * tm
    te = jnp.sum(tile_start[:, None] >= ends[None, :], axis=1).astype(jnp.int32)
    last_valid = jnp.maximum(ends[-1] // tm - 1, 0)
    te = jnp.where(tile_start < ends[-1], te, te[last_valid])
    te = jnp.minimum(te, N_EXPERTS - 1).astype(jnp.int32)
    tf = jnp.concatenate([jnp.ones((1,), jnp.int32), (te[1:] != te[:-1]).astype(jnp.int32)])
    return src, gate_sorted, pos_a, pos_b, te, tf


def _layer0(h, mod, pos, w, bufs):
    n_seq, L, D = h.shape
    cos, sin = _rope_tables(pos)
    xn = norm_mod(h, w['norm_mix_g'][0], mod, 0, 1)
    qkv = proj_rope(xn, w['a_w_qkv'][0], n_seq, L, cos, sin, lambda j: (j % 3) != 2, tn=A_COLS, name="a_qkv")
    outs, lses = [], []
    for g, (_, dil) in enumerate(A_GROUPS):
        if bufs is None:
            o, lse = band_attention_a(qkv, g, dil)
        else:
            o, lse = sample_attention_a(qkv, bufs[g], g, dil)
        outs.append(o)
        lses.append(lse)
    h = merge_wo(outs, lses, w['a_w_o'][0], h, mod, 2)
    xn = norm_mod(h, w['norm_ffn_g'][0], mod, 3, 4)
    act = swiglu_in(xn, w['ffn_w_in'][0], D_FF, name="ffn_in")
    h = proj_residual(act, w['ffn_w_out'][0], h, mod, 5, name="ffn_out")
    kv_new = [qkv[:, :, g * 3 * A_COLS + A_COLS:(g + 1) * 3 * A_COLS] for g in range(len(A_GROUPS))]
    return h, kv_new


def _layer1_mixer(h, mod, kvmod, pos, w, past):
    n_seq, L, D = h.shape
    cos, sin = _rope_tables(pos)
    z = norm_mod(h, w['kv_norm_g'], kvmod, 0, 1)
    kvm = proj_rope(z, w['b_w_kv'], n_seq, L, cos, sin, lambda j: (j == 2) | (j == 4), tn=CMP_PLANE, name="b_kv")
    rows, win = kvm[:, :, :4 * CMP_PLANE], kvm[:, :, 4 * CMP_PLANE:]
    xn = norm_mod(h, w['norm_mix_g'][1], mod, 0, 1)
    n_q = B_HEADS * HEAD_DIM
    q = proj_rope(xn, w['b_w_qg'][0], n_seq, L, cos, sin, lambda j: j >= 0, tn=512, n_cols=n_q, name="b_q")
    wg = jnp.pad(w['b_w_qg'][0][:, n_q:], ((0, 0), (0, LANES - N_BRANCH * B_HEADS)))
    gates = proj_rope(xn, wg, n_seq, L, cos, sin, None, tn=LANES, sigmoid_out=True, name="b_gates")
    if past is None:
        pages = rows.reshape(n_seq * PAGES_PER_SEQ, PAGE_SIZE, 4 * CMP_PLANE)
        table = jnp.arange(n_seq * PAGES_PER_SEQ, dtype=jnp.int32).reshape(n_seq, PAGES_PER_SEQ)
        kc, vc = compress_pages(pages, table, w)
        gr = gates[:, :, :N_BRANCH * B_HEADS].reshape(n_seq, L, B_KV_HEADS, B_HPG * N_BRANCH).transpose(0, 2, 1, 3)
        o = nsa_prompt(q, gr, kc, vc, rows, win)
    else:
        pages, table, win_buf, past_len = past
        kc, vc = compress_pages(pages, table, w)
        o = nsa_sample(q, gates, kc, vc, pages, table, rows, win_buf, win, past_len)
    h = proj_residual(o.reshape(n_seq * L, n_q).astype(BF16), w['b_w_o'][0], h, mod, 2, name="b_out")
    return h, rows, win


def kernel(x_prompt, x_sample, c_prompt, c_sample, cache_a_w128, cache_a_w512, cache_a_w2048, cache_b_kv,
           cache_b_win, page_table, norm_mix_g, norm_ffn_g, mod_w, mod_b, a_w_qkv, a_w_o, kv_norm_g, kv_mod_w,
           kv_mod_b, b_w_kv, cmp_w1_k, cmp_w2_k, cmp_pe_k, cmp_w1_v, cmp_w2_v, cmp_pe_v, b_w_qg, b_w_o,
           ffn_w_in, ffn_w_out, moe_w_router, moe_b_router, moe_w_in, moe_w_out, final_norm_g):
    w = {'norm_mix_g': norm_mix_g, 'norm_ffn_g': norm_ffn_g, 'a_w_qkv': a_w_qkv, 'a_w_o': a_w_o,
         'kv_norm_g': kv_norm_g, 'b_w_kv': b_w_kv, 'cmp_w1_k': cmp_w1_k, 'cmp_w2_k': cmp_w2_k,
         'cmp_pe_k': cmp_pe_k, 'cmp_w1_v': cmp_w1_v, 'cmp_w2_v': cmp_w2_v, 'cmp_pe_v': cmp_pe_v,
         'b_w_qg': b_w_qg, 'b_w_o': b_w_o, 'ffn_w_in': ffn_w_in, 'ffn_w_out': ffn_w_out}
    B, L, D = x_prompt.shape
    S, Q, _ = x_sample.shape
    past_len = page_table.shape[1] * PAGE_SIZE
    caches_a = (cache_a_w128, cache_a_w512, cache_a_w2048)

    n_c = B + S
    c_all = jnp.pad(jnp.concatenate([c_prompt, c_sample], axis=0), ((0, (-n_c) % SUBLANES), (0, 0)))
    mods = [mod_proj(c_all, mod_w[l], mod_b[l]) for l in range(2)]
    kvmod = mod_proj(c_all, kv_mod_w, kv_mod_b)

    def split(m):
        return m[:B].reshape(B, 1, -1), m[B:n_c].reshape(S, 1, -1)

    mod_p, mod_s = zip(*[split(m) for m in mods])
    kvmod_p, kvmod_s = split(kvmod)
    pos_p = jnp.arange(L, dtype=jnp.int32)
    pos_s = past_len + jnp.arange(Q, dtype=jnp.int32)

    bufs = [c[0].reshape(S, c.shape[2], 2 * A_COLS) for c in caches_a]
    h_p, kv_p = _layer0(x_prompt, mod_p[0], pos_p, w, None)
    h_s, kv_s = _layer0(x_sample, mod_s[0], pos_s, w, bufs)
    a_out = []
    for g, (window, _) in enumerate(A_GROUPS):
        keep_p = min(window, L)
        a_p = kv_p[g][:, L - keep_p:].reshape(1, B, keep_p, 2, A_HEADS, HEAD_DIM)
        all_s = jnp.concatenate([bufs[g], kv_s[g]], axis=1)
        keep_s = min(window, all_s.shape[1])
        a_s = all_s[:, all_s.shape[1] - keep_s:].reshape(1, S, keep_s, 2, A_HEADS, HEAD_DIM)
        a_out += [a_p, a_s]

    pages = cache_b_kv.reshape(cache_b_kv.shape[0], PAGE_SIZE, 4 * CMP_PLANE)
    win_buf = cache_b_win.reshape(S, cache_b_win.shape[1], 2 * CMP_PLANE)
    h_p, rows_p, win_p = _layer1_mixer(h_p, mod_p[1], kvmod_p, pos_p, w, None)
    h_s, rows_s, win_s = _layer1_mixer(h_s, mod_s[1], kvmod_s, pos_s, w, (pages, page_table, win_buf, past_len))
    bkv_p = rows_p.reshape(B, L, 4, B_KV_HEADS, HEAD_DIM)
    bkv_s = rows_s.reshape(S, Q, 4, B_KV_HEADS, HEAD_DIM)
    keep = min(B_WINDOW, L)
    bwin_p = win_p[:, L - keep:].reshape(B, keep, 2, B_KV_HEADS, HEAD_DIM)
    win_all = jnp.concatenate([win_buf, win_s], axis=1)
    keep = min(B_WINDOW, win_all.shape[1])
    bwin_s = win_all[:, win_all.shape[1] - keep:].reshape(S, keep, 2, B_KV_HEADS, HEAD_DIM)

    xn = jnp.concatenate([norm_mod(h_p, norm_ffn_g[1], mod_p[1], 3, 4),
                          norm_mod(h_s, norm_ffn_g[1], mod_s[1], 3, 4)], axis=0)
    T = xn.shape[0]
    gate, sel = moe_router(xn, moe_w_router[0], moe_b_router[0])
    src, gate_sorted, pos_a, pos_b, te, tf = _moe_plan(gate, sel, MOE_TILE)
    packed = lax.bitcast_convert_type(xn.reshape(T, D // 2, 2), jnp.uint32)
    x_sorted = lax.bitcast_convert_type(gather_rows(packed, src), BF16).reshape(src.shape[0], D)
    eo = moe_experts(x_sorted, moe_w_in[0], moe_w_out[0], te, tf, gate_sorted)
    y_p = moe_combine_norm(eo, pos_a, pos_b, h_p, mod_p[1], 5, final_norm_g, 0)
    y_s = moe_combine_norm(eo, pos_a, pos_b, h_s, mod_s[1], 5, final_norm_g, B * L)

    return (y_p, y_s, a_out[0], a_out[1], a_out[2], a_out[3], a_out[4], a_out[5], bkv_p, bkv_s, bwin_p, bwin_s)
```

```python
import functools
import math

import numpy as np
import jax
import jax.numpy as jnp
from jax import lax
from jax.experimental import pallas as pl
from jax.experimental.pallas import tpu as pltpu

F32 = jnp.float32
BF16 = jnp.bfloat16

D_MODEL = 2048
HEAD_DIM = 128
ROPE_THETA = 10000.0
NORM_EPS = 1e-6
A_GROUPS = ((128, 1), (512, 4), (2048, 16))
A_HEADS = 8
A_COLS = A_HEADS * HEAD_DIM
B_HEADS = 16
B_KV_HEADS = 4
B_HPG = B_HEADS // B_KV_HEADS
N_BRANCH = 3
CMP_LEN = 32
CMP_STRIDE = 16
CMP_HIDDEN = 256
SLC_LEN = 64
N_SELECT = 16
B_WINDOW = 512
D_FF = 5632
N_EXPERTS = 8
TOP_K = 2
D_FF_EXPERT = 7168
PAGE_SIZE = 128
Q_BLOCK = 128
NEG_INF = -1e30
FORCE_SCORE = 1e9
LANES = 128
SUBLANES = 8
VMEM_LIMIT = 56 * 1024 * 1024
ATT_SCALE = HEAD_DIM ** -0.5


def _cparams(n_axes, vmem=VMEM_LIMIT):
    return pltpu.CompilerParams(dimension_semantics=("arbitrary",) * n_axes, vmem_limit_bytes=vmem)


def _silu(x):
    return x * jax.nn.sigmoid(x)


def _masked_softmax(s, mask):
    s = jnp.where(mask, s, NEG_INF)
    m = jnp.max(s, axis=-1, keepdims=True)
    e = jnp.where(mask, jnp.exp(s - m), 0.0)
    den = jnp.maximum(jnp.sum(e, axis=-1, keepdims=True), 1e-30)
    return e / den, m + jnp.log(den)


def _dot(a, b):
    return jnp.dot(a, b, preferred_element_type=F32)


def _dot_t(a, b):
    return lax.dot_general(a, b, (((1,), (1,)), ((), ())), preferred_element_type=F32)


def _rope_tables(pos):
    half = HEAD_DIM // 2
    inv = ROPE_THETA ** (-jnp.arange(half, dtype=F32) / half)
    ang = pos.astype(F32)[:, None] * inv[None, :]
    cos, sin = jnp.cos(ang), jnp.sin(ang)
    return jnp.concatenate([cos, cos], axis=-1), jnp.concatenate([-sin, sin], axis=-1)


def _rope_head(x, c, s):
    return x * c + pltpu.roll(x, HEAD_DIM // 2, axis=x.ndim - 1) * s


def _norm_mod_kernel(x_ref, g_ref, sh_ref, sc_ref, o_ref):
    x = x_ref[...]
    y = x * lax.rsqrt(jnp.mean(x * x, axis=-1, keepdims=True) + NORM_EPS) * g_ref[...]
    y = y * (1.0 + sc_ref[...]) + sh_ref[...]
    o_ref[...] = y.reshape(o_ref.shape).astype(o_ref.dtype)


def _seq_tiles(n_seq, seq_len, rows):
    if seq_len >= rows:
        return 1, rows
    return min(n_seq, rows // seq_len), seq_len


def norm_mod(h, g, mod, shift_col, scale_col, rows=256, out_dtype=BF16):
    n_seq, L, D = h.shape
    bs, tl = _seq_tiles(n_seq, L, rows)
    nl = L // tl
    return pl.pallas_call(
        _norm_mod_kernel,
        grid=(n_seq // bs, nl),
        in_specs=[
            pl.BlockSpec((bs, tl, D), lambda b, l: (b, l, 0)),
            pl.BlockSpec((1, D), lambda b, l: (0, 0)),
            pl.BlockSpec((bs, 1, D), lambda b, l: (b, 0, shift_col)),
            pl.BlockSpec((bs, 1, D), lambda b, l: (b, 0, scale_col)),
        ],
        out_specs=pl.BlockSpec((bs * tl, D), lambda b, l: (b * nl + l, 0)),
        out_shape=jax.ShapeDtypeStruct((n_seq * L, D), out_dtype),
        compiler_params=_cparams(2),
        name="norm_mod",
    )(h, g.reshape(1, D), mod, mod)


def fused_matmul(x, weights, w_specs, extras, extra_specs, out_shapes, out_specs, epilogue, *,
                 grid, x_spec, w_block, prologue=None, prefetch=(), recast=None, name="matmul"):
    nw, ne, no, npf = len(weights), len(extras), len(out_shapes), len(prefetch)

    def body(*refs):
        pf = refs[:npf]
        x_ref = refs[npf]
        w_refs = refs[npf + 1:npf + 1 + nw]
        e_refs = refs[npf + 1 + nw:npf + 1 + nw + ne]
        o_refs = refs[npf + 1 + nw + ne:npf + 1 + nw + ne + no]
        wb = refs[npf + 1 + nw + ne + no:]
        j, i = pl.program_id(0), pl.program_id(1)
        cond = (i == 0) if recast is None else recast(j, i, *pf)

        @pl.when(cond)
        def _():
            for k in range(nw):
                wb[k][...] = w_refs[k][...].astype(BF16)

        xv = x_ref[...]
        if prologue is not None:
            xv = prologue(xv)
        accs = [_dot(xv, wb[k][...]) for k in range(nw)]
        epilogue(accs, e_refs, o_refs, j, i, pf)

    gs = pltpu.PrefetchScalarGridSpec(
        num_scalar_prefetch=npf,
        grid=grid,
        in_specs=[x_spec] + list(w_specs) + list(extra_specs),
        out_specs=list(out_specs),
        scratch_shapes=[pltpu.VMEM(w_block, BF16) for _ in range(nw)],
    )
    return pl.pallas_call(body, grid_spec=gs, out_shape=list(out_shapes),
                          compiler_params=_cparams(2), name=name)(*prefetch, x, *weights, *extras)


def mod_proj(c, w, b, tn=1024):
    n, K = c.shape
    N = w.shape[-1]

    def epi(accs, e_refs, o_refs, j, i, pf):
        o_refs[0][...] = accs[0] + e_refs[0][...]

    return fused_matmul(
        c, [w], [pl.BlockSpec((K, tn), lambda j, i: (0, j))],
        [b.reshape(1, N)], [pl.BlockSpec((1, tn), lambda j, i: (0, j))],
        [jax.ShapeDtypeStruct((n, N), F32)], [pl.BlockSpec((n, tn), lambda j, i: (0, j))],
        epi, grid=(N // tn, 1), x_spec=pl.BlockSpec((n, K), lambda j, i: (0, 0)),
        w_block=(K, tn), prologue=lambda v: _silu(v).astype(BF16), name="mod_proj")[0]


def proj_rope(xn, w, n_seq, L, cos, sin, rope_pred, *, tn, rows=512, n_cols=None, sigmoid_out=False,
              name="proj"):
    K = xn.shape[1]
    N = w.shape[-1] if n_cols is None else n_cols
    bs, tl = _seq_tiles(n_seq, L, rows)
    nl = L // tl
    tm = bs * tl
    n_heads = tn // HEAD_DIM

    def epi(accs, e_refs, o_refs, j, i, pf):
        acc = accs[0]
        if rope_pred is None:
            if sigmoid_out:
                acc = jax.nn.sigmoid(acc)
            o_refs[0][...] = acc.reshape(bs, tl, tn)
            return
        do_rope = rope_pred(j)

        @pl.when(do_rope)
        def _():
            c = e_refs[0][...][None]
            s = e_refs[1][...][None]
            for h in range(n_heads):
                hs = slice(h * HEAD_DIM, (h + 1) * HEAD_DIM)
                a = acc[:, hs]
                r = pltpu.roll(a, HEAD_DIM // 2, axis=1)
                o_refs[0][:, :, hs] = a.reshape(bs, tl, HEAD_DIM) * c + r.reshape(bs, tl, HEAD_DIM) * s

        @pl.when(jnp.logical_not(do_rope))
        def _():
            o_refs[0][...] = acc.reshape(bs, tl, tn)

    w_spec = pl.BlockSpec((K, tn), lambda j, i: (0, j))
    tab_spec = pl.BlockSpec((tl, HEAD_DIM), lambda j, i: (i % nl, 0))
    return fused_matmul(
        xn, [w], [w_spec], [cos, sin], [tab_spec, tab_spec],
        [jax.ShapeDtypeStruct((n_seq, L, N), F32)],
        [pl.BlockSpec((bs, tl, tn), lambda j, i: (i // nl, i % nl, j))],
        epi, grid=(N // tn, (n_seq * L) // tm), x_spec=pl.BlockSpec((tm, K), lambda j, i: (i, 0)),
        w_block=(K, tn), name=name)[0]


def proj_residual(x, w, h, mod, gate_col, *, tn=512, rows=512, name="proj_res"):
    n_seq, L, N = h.shape
    K = x.shape[1]
    bs, tl = _seq_tiles(n_seq, L, rows)
    nl = L // tl
    tm = bs * tl
    ncb = N // tn

    def epi(accs, e_refs, o_refs, j, i, pf):
        o_refs[0][...] = e_refs[0][...] + e_refs[1][...] * accs[0].reshape(bs, tl, tn)

    blk = pl.BlockSpec((bs, tl, tn), lambda j, i: (i // nl, i % nl, j))
    return fused_matmul(
        x, [w], [pl.BlockSpec((K, tn), lambda j, i: (0, j))],
        [h, mod], [blk, pl.BlockSpec((bs, 1, tn), lambda j, i: (i // nl, 0, gate_col * ncb + j))],
        [jax.ShapeDtypeStruct((n_seq, L, N), F32)], [blk],
        epi, grid=(ncb, (n_seq * L) // tm), x_spec=pl.BlockSpec((tm, K), lambda j, i: (i, 0)),
        w_block=(K, tn), name=name)[0]


def swiglu_in(xn, w_in, d_ff, *, tn=512, tm=512, name="swiglu_in"):
    M, K = xn.shape
    ncb = d_ff // tn
    tm = min(tm, M)

    def epi(accs, e_refs, o_refs, j, i, pf):
        o_refs[0][...] = (_silu(accs[0]) * accs[1]).astype(BF16)

    return fused_matmul(
        xn, [w_in, w_in],
        [pl.BlockSpec((K, tn), lambda j, i: (0, j)), pl.BlockSpec((K, tn), lambda j, i: (0, ncb + j))],
        [], [], [jax.ShapeDtypeStruct((M, d_ff), BF16)], [pl.BlockSpec((tm, tn), lambda j, i: (i, j))],
        epi, grid=(ncb, M // tm), x_spec=pl.BlockSpec((tm, K), lambda j, i: (i, 0)),
        w_block=(K, tn), name=name)[0]


def _band_kernel(q_ref, kc_ref, kp_ref, vc_ref, vp_ref, o_ref, lse_ref, *, n_back):
    i = pl.program_id(2)
    qb = q_ref.shape[0]
    qi = lax.broadcasted_iota(jnp.int32, (qb, 2 * qb), 0)
    kj = lax.broadcasted_iota(jnp.int32, (qb, 2 * qb), 1)
    rel = qi + qb - kj
    mask = (rel >= 0) & (rel <= n_back) & ((kj >= qb) | (i > 0))
    lane = lax.broadcasted_iota(jnp.int32, (qb, A_HEADS), 1)
    lse_all = jnp.zeros((qb, A_HEADS), F32)
    for h in range(A_HEADS):
        hs = slice(h * HEAD_DIM, (h + 1) * HEAD_DIM)
        q = q_ref[:, hs].astype(BF16)
        k = jnp.concatenate([kp_ref[:, hs], kc_ref[:, hs]], axis=0).astype(BF16)
        v = jnp.concatenate([vp_ref[:, hs], vc_ref[:, hs]], axis=0).astype(BF16)
        s = _dot_t(q, k) * ATT_SCALE
        p, lse = _masked_softmax(s, mask)
        o_ref[:, hs] = _dot(p.astype(BF16), v)
        lse_all = jnp.where(lane == h, lse, lse_all)
    lse_ref[...] = lse_all


def band_attention_a(qkv, g, dil):
    B, L, NC = qkv.shape
    n_back = A_GROUPS[g][0] // dil
    assert n_back == Q_BLOCK
    Lr = L // dil
    qb = math.gcd(Lr, Q_BLOCK)
    nb = Lr // qb
    ncb = NC // A_COLS
    view = qkv.reshape(B, Lr, dil * NC)

    def col(part):
        return lambda b, r, i: (b, i, r * ncb + g * 3 + part)

    def col_prev(part):
        return lambda b, r, i: (b, jnp.maximum(i - 1, 0), r * ncb + g * 3 + part)

    blk = (None, qb, A_COLS)
    o, lse = pl.pallas_call(
        functools.partial(_band_kernel, n_back=n_back),
        grid=(B, dil, nb),
        in_specs=[pl.BlockSpec(blk, col(0)), pl.BlockSpec(blk, col(1)), pl.BlockSpec(blk, col_prev(1)),
                  pl.BlockSpec(blk, col(2)), pl.BlockSpec(blk, col_prev(2))],
        out_specs=[pl.BlockSpec(blk, lambda b, r, i: (b, i, r)),
                   pl.BlockSpec((None, None, qb, A_HEADS), lambda b, r, i: (b, r, i, 0))],
        out_shape=[jax.ShapeDtypeStruct((B, Lr, dil * A_COLS), F32),
                   jax.ShapeDtypeStruct((B, dil, Lr, A_HEADS), F32)],
        compiler_params=_cparams(3),
        name="band_attention_a",
    )(view, view, view, view, view)
    return o.reshape(B, L, A_COLS), lse.transpose(0, 2, 1, 3).reshape(B, L, A_HEADS)


def _dilated_kernel(*refs, dil, has_prev, heads, qc):
    if has_prev:
        q_ref, kc_ref, kp_ref, vc_ref, vp_ref, o_ref, lse_ref = refs
    else:
        q_ref, kc_ref, vc_ref, o_ref, lse_ref = refs
    i = pl.program_id(2)
    qb = q_ref.shape[0]
    span = dil * Q_BLOCK
    lane = lax.broadcasted_iota(jnp.int32, (qc, heads), 1)
    for c in range(qb // qc):
        n_prev = qb - c * qc if has_prev else 0
        n_keys = n_prev + (c + 1) * qc
        a = lax.broadcasted_iota(jnp.int32, (qc, n_keys), 0)
        x = lax.broadcasted_iota(jnp.int32, (qc, n_keys), 1)
        rel = a + (c * qc + n_prev) - x
        mask = (rel >= 0) & (rel <= span) & ((rel % dil) == 0)
        if has_prev:
            mask = mask & ((x >= n_prev) | (i > 0))
        lse_c = jnp.zeros((qc, heads), F32)
        for h in range(heads):
            hs = slice(h * HEAD_DIM, (h + 1) * HEAD_DIM)
            q = q_ref[c * qc:(c + 1) * qc, hs].astype(BF16)
            k = kc_ref[:(c + 1) * qc, hs]
            v = vc_ref[:(c + 1) * qc, hs]
            if has_prev:
                k = jnp.concatenate([kp_ref[c * qc:, hs], k], axis=0)
                v = jnp.concatenate([vp_ref[c * qc:, hs], v], axis=0)
            s = _dot_t(q, k.astype(BF16)) * ATT_SCALE
            p, lse = _masked_softmax(s, mask)
            o_ref[c * qc:(c + 1) * qc, hs] = _dot(p.astype(BF16), v.astype(BF16))
            lse_c = jnp.where(lane == h, lse, lse_c)
        lse_ref[c * qc:(c + 1) * qc, :] = lse_c


def dilated_attention_a(qkv, g, dil, qc=256):
    B, L, NC = qkv.shape
    assert A_GROUPS[g][0] // dil == Q_BLOCK
    qb = min(dil * Q_BLOCK, L)
    nb = L // qb
    has_prev = nb > 1
    heads = A_HEADS if has_prev else 2
    hc = heads * HEAD_DIM
    per = A_COLS // hc

    def col(part):
        return pl.BlockSpec((None, qb, hc), lambda b, hb, i: (b, i, (g * 3 + part) * per + hb))

    def col_prev(part):
        return pl.BlockSpec((None, qb, hc), lambda b, hb, i: (b, jnp.maximum(i - 1, 0), (g * 3 + part) * per + hb))

    if has_prev:
        in_specs = [col(0), col(1), col_prev(1), col(2), col_prev(2)]
    else:
        in_specs = [col(0), col(1), col(2)]
    o, lse = pl.pallas_call(
        functools.partial(_dilated_kernel, dil=dil, has_prev=has_prev, heads=heads, qc=qc),
        grid=(B, per, nb),
        in_specs=in_specs,
        out_specs=[pl.BlockSpec((None, qb, hc), lambda b, hb, i: (b, i, hb)),
                   pl.BlockSpec((None, None, qb, heads), lambda b, hb, i: (b, hb, i, 0))],
        out_shape=[jax.ShapeDtypeStruct((B, L, A_COLS), F32),
                   jax.ShapeDtypeStruct((B, per, L, heads), F32)],
        compiler_params=_cparams(3),
        name="dilated_attention_a",
    )(*([qkv] * len(in_specs)))
    return o, lse.transpose(0, 2, 1, 3).reshape(B, L, A_HEADS)


def _sample_a_kernel(qkv_ref, b0_ref, b1_ref, b2_ref, o_ref):
    bufs = (b0_ref, b1_ref, b2_ref)
    n_parts = 3 * A_HEADS
    tok_rows = len(A_GROUPS) * n_parts
    n_q = qkv_ref.shape[0] // tok_rows
    for qi in range(n_q):
        outs, lses = [], []
        for g, (window, dil) in enumerate(A_GROUPS):
            def new_tile(n, part):
                r0 = n * tok_rows + g * n_parts + part * A_HEADS
                return qkv_ref[r0:r0 + A_HEADS, :]

            q = new_tile(qi, 0)
            res, m0 = qi % dil, qi // dil
            r0 = res * 2 * A_HEADS
            new_toks = [n for n in range(qi + 1) if (qi - n) % dil == 0]
            k = jnp.concatenate([bufs[g][m0:, r0:r0 + A_HEADS, :]] + [new_tile(n, 1)[None] for n in new_toks], axis=0)
            v = jnp.concatenate([bufs[g][m0:, r0 + A_HEADS:r0 + 2 * A_HEADS, :]]
                                + [new_tile(n, 2)[None] for n in new_toks], axis=0)
            s = jnp.sum(k * q[None], axis=-1, keepdims=True) * ATT_SCALE
            m = jnp.max(s, axis=0, keepdims=True)
            e = jnp.exp(s - m)
            den = jnp.sum(e, axis=0, keepdims=True)
            outs.append(jnp.sum((e / den) * v, axis=0))
            lses.append((m + jnp.log(den))[0])
        mx = jnp.maximum(jnp.maximum(lses[0], lses[1]), lses[2])
        es = [jnp.exp(l - mx) for l in lses]
        den = es[0] + es[1] + es[2]
        o_ref[qi * A_HEADS:(qi + 1) * A_HEADS, :] = sum((e / den) * o for e, o in zip(es, outs))


def sample_attention_a(qkv_rows, caches):
    S, n_rows, _ = qkv_rows.shape
    n_q = n_rows // (len(A_GROUPS) * 3 * A_HEADS)
    views, specs = [], []
    for cache, (window, dil) in zip(caches, A_GROUPS):
        assert cache.shape[2] == window and window // dil == Q_BLOCK
        n_res = min(dil, n_q)
        views.append(cache.reshape(S, window // dil, dil * 2 * A_HEADS, HEAD_DIM))
        specs.append(pl.BlockSpec((None, window // dil, n_res * 2 * A_HEADS, HEAD_DIM), lambda s: (s, 0, 0, 0)))
    return pl.pallas_call(
        _sample_a_kernel,
        grid=(S,),
        in_specs=[pl.BlockSpec((None, n_rows, HEAD_DIM), lambda s: (s, 0, 0))] + specs,
        out_specs=pl.BlockSpec((None, n_q * A_HEADS, HEAD_DIM), lambda s: (s, 0, 0)),
        out_shape=jax.ShapeDtypeStruct((S, n_q * A_HEADS, HEAD_DIM), F32),
        compiler_params=_cparams(1),
        name="sample_attention_a",
    )(qkv_rows, *views)


def _shift_append_kernel(buf_hbm, new_hbm, out_hbm, sem):
    s = pl.program_id(0)
    n_keep = buf_hbm.shape[1] - new_hbm.shape[1]

    def copies(seq, slot):
        old = pltpu.make_async_copy(buf_hbm.at[seq, pl.ds(new_hbm.shape[1], n_keep), :],
                                    out_hbm.at[seq, pl.ds(0, n_keep), :], sem.at[slot, 0])
        new = pltpu.make_async_copy(new_hbm.at[seq], out_hbm.at[seq, pl.ds(n_keep, new_hbm.shape[1]), :],
                                    sem.at[slot, 1])
        return old, new

    for c in copies(s, s % 2):
        c.start()

    @pl.when(s > 0)
    def _():
        for c in copies(s - 1, (s - 1) % 2):
            c.wait()

    @pl.when(s == pl.num_programs(0) - 1)
    def _():
        for c in copies(s, s % 2):
            c.wait()


def shift_append(buf, new):
    return pl.pallas_call(
        _shift_append_kernel,
        grid=(buf.shape[0],),
        in_specs=[pl.BlockSpec(memory_space=pl.ANY), pl.BlockSpec(memory_space=pl.ANY)],
        out_specs=pl.BlockSpec(memory_space=pl.ANY),
        out_shape=jax.ShapeDtypeStruct(buf.shape, buf.dtype),
        scratch_shapes=[pltpu.SemaphoreType.DMA((2, 2))],
        compiler_params=_cparams(1),
        name="shift_append",
    )(buf, new)


def _merge_wo_kernel(o0_ref, o1_ref, o2_ref, l0_ref, l1_ref, l2_ref, w_ref, h_ref, gt_ref, out_ref, wb):
    @pl.when(pl.program_id(0) == 0)
    def _():
        wb[...] = w_ref[...].astype(BF16)

    bs, tl, N = h_ref.shape
    ls = [l0_ref[...], l1_ref[...], l2_ref[...]]
    mx = jnp.maximum(jnp.maximum(ls[0], ls[1]), ls[2])
    es = [jnp.exp(l - mx) for l in ls]
    den = es[0] + es[1] + es[2]
    ws = [e / den for e in es]
    o_refs = (o0_ref, o1_ref, o2_ref)
    parts = []
    for h in range(A_HEADS):
        hs = slice(h * HEAD_DIM, (h + 1) * HEAD_DIM)
        acc = ws[0][:, h:h + 1] * o_refs[0][:, hs]
        for g in (1, 2):
            acc = acc + ws[g][:, h:h + 1] * o_refs[g][:, hs]
        parts.append(acc.astype(BF16))
    merged = jnp.concatenate(parts, axis=1)
    y = _dot(merged, wb[...])
    out_ref[...] = h_ref[...] + gt_ref[...] * y.reshape(bs, tl, N)


def merge_wo(os_, lses, w_o, h, mod, gate_col, rows=256):
    n_seq, L, N = h.shape
    bs, tl = _seq_tiles(n_seq, L, rows)
    nl = L // tl
    tm = bs * tl
    K = w_o.shape[0]
    o_spec = pl.BlockSpec((tm, K), lambda i: (i, 0))
    l_spec = pl.BlockSpec((tm, A_HEADS), lambda i: (i, 0))
    blk = pl.BlockSpec((bs, tl, N), lambda i: (i // nl, i % nl, 0))
    return pl.pallas_call(
        _merge_wo_kernel,
        grid=((n_seq * L) // tm,),
        in_specs=[o_spec] * 3 + [l_spec] * 3 + [
            pl.BlockSpec((K, N), lambda i: (0, 0)), blk,
            pl.BlockSpec((bs, 1, N), lambda i: (i // nl, 0, gate_col))],
        out_specs=blk,
        out_shape=jax.ShapeDtypeStruct((n_seq, L, N), F32),
        scratch_shapes=[pltpu.VMEM((K, N), BF16)],
        compiler_params=_cparams(1),
        name="merge_wo",
    )(*[o.reshape(n_seq * L, K) for o in os_], *[l.reshape(n_seq * L, A_HEADS) for l in lses], w_o, h, mod)


PAGES_PER_SEQ = 16
CHUNKS_PER_PAGE = PAGE_SIZE // CMP_STRIDE
N_CHUNK = PAGES_PER_SEQ * CHUNKS_PER_PAGE
CMP_PLANE = B_KV_HEADS * HEAD_DIM
HALF_FLAT = CMP_STRIDE * HEAD_DIM


def _compress_kernel(pt_ref, *refs):
    pages = refs[:PAGES_PER_SEQ]
    (w1k_ref, w2k_ref, pek_ref, w1v_ref, w2v_ref, pev_ref, cos_ref, sin_ref,
     kc_ref, vc_ref, w1k_b, w1v_b, w2k_b, w2v_b, pe_hid) = refs[PAGES_PER_SEQ:]

    @pl.when(pl.program_id(0) == 0)
    def _():
        w1k_b[...] = w1k_ref[...].astype(BF16)
        w1v_b[...] = w1v_ref[...].astype(BF16)
        w2k_b[...] = w2k_ref[...].astype(BF16)
        w2v_b[...] = w2v_ref[...].astype(BF16)
        for plane, (pe_ref, w1_b) in enumerate(((pek_ref, w1k_b), (pev_ref, w1v_b))):
            flat = jnp.concatenate([pe_ref[r:r + 1, :] for r in range(CMP_LEN)], axis=1)
            flat = jnp.broadcast_to(flat, (SUBLANES, CMP_LEN * HEAD_DIM)).astype(BF16)
            pe_hid[plane] = _dot(flat, w1_b[...])

    pair = 2 * CMP_STRIDE
    n_pairs = CHUNKS_PER_PAGE // 2
    for plane, (w1_b, w2_b, out_ref) in enumerate(((w1k_b, w2k_b, kc_ref), (w1v_b, w2v_b, vc_ref))):
        js = slice(plane * B_KV_HEADS, (plane + 1) * B_KV_HEADS)
        cols = []
        for r in range(CMP_STRIDE):
            tiles = []
            for pg in pages:
                even = pg[pl.ds(r, n_pairs, stride=pair), js, :]
                odd = pg[pl.ds(r + CMP_STRIDE, n_pairs, stride=pair), js, :]
                tiles.append(jnp.concatenate([even, odd], axis=1))
            cols.append(jnp.concatenate(tiles, axis=0).reshape(N_CHUNK * B_KV_HEADS, HEAD_DIM).astype(BF16))
        lhs = jnp.concatenate(cols, axis=1)
        first = _dot(lhs, w1_b[:HALF_FLAT, :])
        second = _dot(lhs, w1_b[HALF_FLAT:, :])
        hid = first + pltpu.roll(second, (N_CHUNK - 1) * B_KV_HEADS, axis=0) + pe_hid[plane][0:1, :]
        out = _dot(_silu(hid).astype(BF16), w2_b[...])
        if plane == 0:
            out = _rope_head(out, cos_ref[...], sin_ref[...])
        out_ref[...] = out


def compress_pages(pages, page_table, w):
    n_seq = page_table.shape[0]
    pos = jnp.arange(N_CHUNK, dtype=jnp.int32) * CMP_STRIDE + CMP_LEN - 1
    cos, sin = _rope_tables(jnp.repeat(pos, B_KV_HEADS))
    n_rows = N_CHUNK * B_KV_HEADS

    def page_spec(p):
        return pl.BlockSpec((None, PAGE_SIZE, 2 * B_KV_HEADS, HEAD_DIM), lambda s, pt: (pt[s, p], 0, 0, 0))

    def const(shape):
        return pl.BlockSpec(shape, lambda s, pt: (0,) * len(shape))

    flat_rows = CMP_LEN * HEAD_DIM
    gs = pltpu.PrefetchScalarGridSpec(
        num_scalar_prefetch=1,
        grid=(n_seq,),
        in_specs=[page_spec(p) for p in range(PAGES_PER_SEQ)] + [
            const((flat_rows, CMP_HIDDEN)), const((CMP_HIDDEN, HEAD_DIM)), const((CMP_LEN, HEAD_DIM)),
            const((flat_rows, CMP_HIDDEN)), const((CMP_HIDDEN, HEAD_DIM)), const((CMP_LEN, HEAD_DIM)),
            const((n_rows, HEAD_DIM)), const((n_rows, HEAD_DIM))],
        out_specs=[pl.BlockSpec((None, n_rows, HEAD_DIM), lambda s, pt: (s, 0, 0))] * 2,
        scratch_shapes=[pltpu.VMEM((flat_rows, CMP_HIDDEN), BF16), pltpu.VMEM((flat_rows, CMP_HIDDEN), BF16),
                        pltpu.VMEM((CMP_HIDDEN, HEAD_DIM), BF16), pltpu.VMEM((CMP_HIDDEN, HEAD_DIM), BF16),
                        pltpu.VMEM((2, SUBLANES, CMP_HIDDEN), F32)],
    )
    return pl.pallas_call(
        _compress_kernel, grid_spec=gs,
        out_shape=[jax.ShapeDtypeStruct((n_seq, n_rows, HEAD_DIM), F32)] * 2,
        compiler_params=_cparams(1), name="compress_pages",
    )(page_table, *([pages] * PAGES_PER_SEQ), w['cmp_w1_k'], w['cmp_w2_k'], w['cmp_pe_k'],
      w['cmp_w1_v'], w['cmp_w2_v'], w['cmp_pe_v'], cos, sin)


def _overlap_matrix(n_slc):
    c0 = np.arange(LANES) * CMP_STRIDE
    s0 = np.arange(LANES) * SLC_LEN
    m = (c0[:, None] < s0[None, :] + SLC_LEN) & (c0[:, None] + CMP_LEN > s0[None, :])
    m &= (np.arange(LANES)[:, None] < N_CHUNK - 1) & (np.arange(LANES)[None, :] < n_slc)
    return jnp.asarray(m.astype(np.float32), dtype=BF16)


def _expand_matrix(n_keys):
    m = (np.arange(n_keys)[None, :] // SLC_LEN) == np.arange(LANES)[:, None]
    return jnp.asarray(m.astype(np.float32), dtype=BF16)


def _select_blocks(p_slc, q_pos, n_slc):
    nq = p_slc.shape[0]
    blk = lax.broadcasted_iota(jnp.int32, (nq, LANES), 1)
    cur = q_pos // SLC_LEN
    forced = (blk == 0) | (blk == cur) | (blk == cur - 1)
    imp = jnp.where(forced, FORCE_SCORE, p_slc)
    imp = jnp.where(blk <= cur, imp, NEG_INF)
    imp = jnp.where(blk < n_slc, imp, -3e38)
    rank = jnp.zeros((nq, LANES), jnp.int32)
    for k in range(n_slc):
        vk = imp[:, k:k + 1]
        ahead = (vk > imp) | ((vk == imp) & (blk > k))
        rank = rank + ahead.astype(jnp.int32)
    sel = (rank < min(N_SELECT, n_slc)) & (blk < n_slc)
    return jnp.where(sel, 1.0, 0.0).astype(BF16)


def _cmp_branch(q, kc, vc, q_pos, overlap, n_q):
    s = _dot_t(q, kc.astype(BF16)) * ATT_SCALE
    n = lax.broadcasted_iota(jnp.int32, (n_q, N_CHUNK), 1)
    mask = (n * CMP_STRIDE + CMP_LEN - 1 <= q_pos) & (n < N_CHUNK - 1)
    p, _ = _masked_softmax(s.reshape(B_HPG, n_q, N_CHUNK), mask[None])
    o = _dot(p.reshape(B_HPG * n_q, N_CHUNK).astype(BF16), vc.astype(BF16))
    p_grp = jnp.sum(p, axis=0)
    return o, _dot(p_grp.astype(BF16), overlap)


def _nsa_prompt_kernel(q_ref, g_ref, kc_ref, vc_ref, ks_ref, vs_ref, kw_ref, vw_ref, ov_ref, ex_ref,
                       o_ref, ks_b, vs_b, kw_b, vw_b):
    i = pl.program_id(2)
    L = ks_ref.shape[0]
    nq = q_ref.shape[0]
    n_slc = L // SLC_LEN

    @pl.when(i == 0)
    def _():
        ks_b[...] = ks_ref[...].astype(BF16)
        vs_b[...] = vs_ref[...].astype(BF16)
        kw_b[...] = kw_ref[...].astype(BF16)
        vw_b[...] = vw_ref[...].astype(BF16)

    q = jnp.concatenate([q_ref[:, j * HEAD_DIM:(j + 1) * HEAD_DIM] for j in range(B_HPG)], axis=0).astype(BF16)
    q_pos = i * nq + lax.broadcasted_iota(jnp.int32, (nq, 1), 0)

    grp = pl.ds(pl.program_id(1), N_CHUNK, stride=B_KV_HEADS)
    o_cmp, p_slc = _cmp_branch(q, kc_ref[grp, :], vc_ref[grp, :], q_pos, ov_ref[...], nq)
    sel = _select_blocks(p_slc, q_pos, n_slc)
    sel_keys = _dot(sel, ex_ref[...])

    k_pos = lax.broadcasted_iota(jnp.int32, (nq, L), 1)
    mask = (sel_keys > 0.5) & (k_pos <= q_pos)
    s = _dot_t(q, ks_b[...]) * ATT_SCALE
    p, _ = _masked_softmax(s.reshape(B_HPG, nq, L), mask[None])
    o_slc = _dot(p.reshape(B_HPG * nq, L).astype(BF16), vs_b[...])

    n_win = B_WINDOW + nq
    start = pl.multiple_of(jnp.maximum(i * nq - B_WINDOW, 0), nq)
    kw = kw_b[pl.ds(start, n_win), :]
    vw = vw_b[pl.ds(start, n_win), :]
    rel = q_pos - (start + lax.broadcasted_iota(jnp.int32, (nq, n_win), 1))
    mask_w = (rel >= 0) & (rel <= B_WINDOW)
    s = _dot_t(q, kw) * ATT_SCALE
    p, _ = _masked_softmax(s.reshape(B_HPG, nq, n_win), mask_w[None])
    o_win = _dot(p.reshape(B_HPG * nq, n_win).astype(BF16), vw)

    gates = g_ref[...]
    for j in range(B_HPG):
        rows = slice(j * nq, (j + 1) * nq)
        c = j * N_BRANCH
        o_ref[:, j * HEAD_DIM:(j + 1) * HEAD_DIM] = (
            gates[:, c:c + 1] * o_cmp[rows] + gates[:, c + 1:c + 2] * o_slc[rows] + gates[:, c + 2:c + 3] * o_win[rows])


def nsa_prompt(q, gates, kc, vc, rows, win):
    B, L, _ = q.shape
    nq = Q_BLOCK
    assert L == N_CHUNK * CMP_STRIDE and L % SLC_LEN == 0
    gw = B_HPG * HEAD_DIM
    full = (None, L, HEAD_DIM)
    return pl.pallas_call(
        _nsa_prompt_kernel,
        grid=(B, B_KV_HEADS, L // nq),
        in_specs=[
            pl.BlockSpec((None, nq, gw), lambda b, g, i: (b, i, g)),
            pl.BlockSpec((None, None, nq, B_HPG * N_BRANCH), lambda b, g, i: (b, g, i, 0)),
            pl.BlockSpec((None, N_CHUNK * B_KV_HEADS, HEAD_DIM), lambda b, g, i: (b, 0, 0)),
            pl.BlockSpec((None, N_CHUNK * B_KV_HEADS, HEAD_DIM), lambda b, g, i: (b, 0, 0)),
            pl.BlockSpec(full, lambda b, g, i: (b, 0, 2 * B_KV_HEADS + g)),
            pl.BlockSpec(full, lambda b, g, i: (b, 0, 3 * B_KV_HEADS + g)),
            pl.BlockSpec(full, lambda b, g, i: (b, 0, g)),
            pl.BlockSpec(full, lambda b, g, i: (b, 0, B_KV_HEADS + g)),
            pl.BlockSpec((LANES, LANES), lambda b, g, i: (0, 0)),
            pl.BlockSpec((LANES, L), lambda b, g, i: (0, 0)),
        ],
        out_specs=pl.BlockSpec((None, nq, gw), lambda b, g, i: (b, i, g)),
        out_shape=jax.ShapeDtypeStruct((B, L, B_HEADS * HEAD_DIM), F32),
        scratch_shapes=[pltpu.VMEM((L, HEAD_DIM), BF16) for _ in range(4)],
        compiler_params=_cparams(3),
        name="nsa_prompt",
    )(q, gates, kc, vc, rows, rows, win, win, _overlap_matrix(L // SLC_LEN), _expand_matrix(L))


def _nsa_sample_kernel(pt_ref, *refs, past_len):
    pages = refs[:PAGES_PER_SEQ]
    (q_ref, g_ref, kc_ref, vc_ref, new_ref, wbuf_ref, wnew_ref, ov_ref, ex_ref, o_ref) = refs[PAGES_PER_SEQ:]
    nq = q_ref.shape[0]
    n_slc = -(-(past_len + nq) // SLC_LEN)
    q_pos = past_len + lax.broadcasted_iota(jnp.int32, (nq, 1), 0)
    qn = lax.broadcasted_iota(jnp.int32, (nq, nq), 0)
    nn = lax.broadcasted_iota(jnp.int32, (nq, nq), 1)
    causal_new = nn <= qn
    n_buf = wbuf_ref.shape[0]
    mb = lax.broadcasted_iota(jnp.int32, (nq, n_buf), 1)
    rel_b = q_pos - (past_len - n_buf + mb)
    mask_wb = (rel_b >= 0) & (rel_b <= B_WINDOW)
    rel_n = qn - nn
    mask_wn = (rel_n >= 0) & (rel_n <= B_WINDOW)
    gates = g_ref[...]

    for g in range(B_KV_HEADS):
        gs = slice(g * HEAD_DIM, (g + 1) * HEAD_DIM)
        vs_ = slice(CMP_PLANE + g * HEAD_DIM, CMP_PLANE + (g + 1) * HEAD_DIM)
        q = jnp.concatenate(
            [q_ref[:, (g * B_HPG + j) * HEAD_DIM:(g * B_HPG + j + 1) * HEAD_DIM] for j in range(B_HPG)],
            axis=0).astype(BF16)
        grp = pl.ds(g, N_CHUNK, stride=B_KV_HEADS)
        o_cmp, p_slc = _cmp_branch(q, kc_ref[grp, :], vc_ref[grp, :], q_pos, ov_ref[...], nq)
        sel = _select_blocks(p_slc, q_pos, n_slc)
        sel_keys = _dot(sel, ex_ref[...])

        k_past = jnp.concatenate([pg[:, g, :] for pg in pages], axis=0).astype(BF16)
        v_past = jnp.concatenate([pg[:, B_KV_HEADS + g, :] for pg in pages], axis=0).astype(BF16)
        k_new = new_ref[:, 2 * CMP_PLANE + g * HEAD_DIM:2 * CMP_PLANE + (g + 1) * HEAD_DIM].astype(BF16)
        v_new = new_ref[:, 3 * CMP_PLANE + g * HEAD_DIM:3 * CMP_PLANE + (g + 1) * HEAD_DIM].astype(BF16)
        mask_p = sel_keys[:, :past_len] > 0.5
        mask_n = (sel_keys[:, past_len:past_len + nq] > 0.5) & causal_new
        o_slc = _two_part_attention(q, k_past, v_past, mask_p, k_new, v_new, mask_n, nq)

        kw = wbuf_ref[:, g, :].astype(BF16)
        vw = wbuf_ref[:, B_KV_HEADS + g, :].astype(BF16)
        kwn = wnew_ref[:, gs].astype(BF16)
        vwn = wnew_ref[:, vs_].astype(BF16)
        o_win = _two_part_attention(q, kw, vw, mask_wb, kwn, vwn, mask_wn, nq)

        for j in range(B_HPG):
            rows = slice(j * nq, (j + 1) * nq)
            hd = g * B_HPG + j
            c = hd * N_BRANCH
            o_ref[:, hd * HEAD_DIM:(hd + 1) * HEAD_DIM] = (
                gates[:, c:c + 1] * o_cmp[rows] + gates[:, c + 1:c + 2] * o_slc[rows]
                + gates[:, c + 2:c + 3] * o_win[rows])


def _two_part_attention(q, k1, v1, mask1, k2, v2, mask2, nq):
    n1, n2 = k1.shape[0], k2.shape[0]
    s1 = jnp.where(mask1[None], (_dot_t(q, k1) * ATT_SCALE).reshape(B_HPG, nq, n1), NEG_INF)
    s2 = jnp.where(mask2[None], (_dot_t(q, k2) * ATT_SCALE).reshape(B_HPG, nq, n2), NEG_INF)
    m = jnp.maximum(jnp.max(s1, axis=-1, keepdims=True), jnp.max(s2, axis=-1, keepdims=True))
    e1 = jnp.where(mask1[None], jnp.exp(s1 - m), 0.0)
    e2 = jnp.where(mask2[None], jnp.exp(s2 - m), 0.0)
    den = jnp.maximum(jnp.sum(e1, axis=-1, keepdims=True) + jnp.sum(e2, axis=-1, keepdims=True), 1e-30)
    o = _dot((e1 / den).reshape(B_HPG * nq, n1).astype(BF16), v1)
    return o + _dot((e2 / den).reshape(B_HPG * nq, n2).astype(BF16), v2)


def nsa_sample(q, gates, kc, vc, pages, page_table, new_rows, win_buf, win_new, past_len):
    S, nq, _ = q.shape
    assert past_len == PAGES_PER_SEQ * PAGE_SIZE and nq <= SLC_LEN
    n_keys = past_len + LANES

    def page_spec(p):
        return pl.BlockSpec((None, PAGE_SIZE, 2 * B_KV_HEADS, HEAD_DIM), lambda s, pt: (pt[s, p], 0, 1, 0))

    def per_seq(rows, cols):
        return pl.BlockSpec((None, rows, cols), lambda s, pt: (s, 0, 0))

    gs = pltpu.PrefetchScalarGridSpec(
        num_scalar_prefetch=1,
        grid=(S,),
        in_specs=[page_spec(p) for p in range(PAGES_PER_SEQ)] + [
            per_seq(nq, B_HEADS * HEAD_DIM), per_seq(nq, LANES), per_seq(N_CHUNK * B_KV_HEADS, HEAD_DIM),
            per_seq(N_CHUNK * B_KV_HEADS, HEAD_DIM), per_seq(nq, 4 * CMP_PLANE),
            pl.BlockSpec((None, win_buf.shape[1], 2 * B_KV_HEADS, HEAD_DIM), lambda s, pt: (s, 0, 0, 0)),
            per_seq(nq, 2 * CMP_PLANE),
            pl.BlockSpec((LANES, LANES), lambda s, pt: (0, 0)),
            pl.BlockSpec((LANES, n_keys), lambda s, pt: (0, 0))],
        out_specs=per_seq(nq, B_HEADS * HEAD_DIM),
    )
    n_slc = -(-(past_len + nq) // SLC_LEN)
    return pl.pallas_call(
        functools.partial(_nsa_sample_kernel, past_len=past_len), grid_spec=gs,
        out_shape=jax.ShapeDtypeStruct((S, nq, B_HEADS * HEAD_DIM), F32),
        compiler_params=_cparams(1), name="nsa_sample",
    )(page_table, *([pages] * PAGES_PER_SEQ), q, gates, kc, vc, new_rows, win_buf, win_new,
      _overlap_matrix(n_slc), _expand_matrix(n_keys))


MOE_TILE = 256


def _router_kernel(x_ref, w_ref, b_ref, gate_ref, sel_ref):
    logits = _dot(x_ref[...].astype(BF16), w_ref[...].astype(BF16)) + b_ref[...]
    lane = lax.broadcasted_iota(jnp.int32, logits.shape, 1)
    logits = jnp.where(lane < N_EXPERTS, logits, -jnp.inf)
    m1 = jnp.max(logits, axis=-1, keepdims=True)
    i1 = jnp.min(jnp.where(logits == m1, lane, LANES), axis=-1, keepdims=True)
    rest = jnp.where(lane == i1, -jnp.inf, logits)
    m2 = jnp.max(rest, axis=-1, keepdims=True)
    i2 = jnp.min(jnp.where(rest == m2, lane, LANES), axis=-1, keepdims=True)
    e2 = jnp.exp(m2 - m1)
    den = 1.0 + e2
    gate_ref[...] = jnp.where(lane == i1, 1.0 / den, jnp.where(lane == i2, e2 / den, 0.0))
    sel_ref[...] = ((lane == i1) | (lane == i2)).astype(jnp.int32)


def moe_router(xn, w_router, b_router, tm=512):
    T, D = xn.shape
    wp = jnp.pad(w_router, ((0, 0), (0, LANES - N_EXPERTS)))
    bp = jnp.pad(b_router, (0, LANES - N_EXPERTS)).reshape(1, LANES)
    return pl.pallas_call(
        _router_kernel,
        grid=(T // tm,),
        in_specs=[pl.BlockSpec((tm, D), lambda i: (i, 0)), pl.BlockSpec((D, LANES), lambda i: (0, 0)),
                  pl.BlockSpec((1, LANES), lambda i: (0, 0))],
        out_specs=[pl.BlockSpec((tm, LANES), lambda i: (i, 0))] * 2,
        out_shape=[jax.ShapeDtypeStruct((T, LANES), F32), jax.ShapeDtypeStruct((T, LANES), jnp.int32)],
        compiler_params=_cparams(1), name="moe_router",
    )(xn, wp, bp)


def _row_copy(src, src_row, dst, dst_row, sem):
    return pltpu.make_async_copy(src.at[pl.ds(src_row, 1), :], dst.at[pl.ds(dst_row, 1), :], sem)


def _dispatch_kernel(pa_ref, pb_ref, x_ref, prev_hbm, out_hbm, sem, *, tok_offset):
    del prev_hbm
    tc = x_ref.shape[0]
    base = tok_offset + pl.program_id(0) * tc

    def start(r, c):
        _row_copy(x_ref, r, out_hbm, pa_ref[base + r], sem.at[0]).start()
        _row_copy(x_ref, r, out_hbm, pb_ref[base + r], sem.at[1]).start()
        return c

    def wait(r, c):
        _row_copy(x_ref, r, out_hbm, 0, sem.at[0]).wait()
        _row_copy(x_ref, r, out_hbm, 0, sem.at[1]).wait()
        return c

    lax.fori_loop(0, tc, start, 0)
    lax.fori_loop(0, tc, wait, 0)


def moe_dispatch(x, pos_a, pos_b, buf, tok_offset, tc=256):
    T, D = x.shape
    gs = pltpu.PrefetchScalarGridSpec(
        num_scalar_prefetch=2, grid=(T // tc,),
        in_specs=[pl.BlockSpec((tc, D), lambda i, pa, pb: (i, 0)), pl.BlockSpec(memory_space=pl.ANY)],
        out_specs=pl.BlockSpec(memory_space=pl.ANY),
        scratch_shapes=[pltpu.SemaphoreType.DMA((2,))],
    )
    return pl.pallas_call(
        functools.partial(_dispatch_kernel, tok_offset=tok_offset), grid_spec=gs,
        out_shape=jax.ShapeDtypeStruct(buf.shape, buf.dtype), input_output_aliases={3: 0},
        compiler_params=_cparams(1), name="moe_dispatch")(pos_a, pos_b, x, buf)


def moe_experts(x_sorted, w_in, w_out, tile_expert, tile_first, tm=MOE_TILE):
    P, K = x_sorted.shape
    F = w_out.shape[1]
    tn1, tn2 = 1024, 512
    ncb = F // tn1

    def recast(j, i, te, tf):
        return tf[i] == 1

    def epi_in(accs, e_refs, o_refs, j, i, pf):
        o_refs[0][...] = (_silu(accs[0]) * accs[1]).astype(BF16)

    act = fused_matmul(
        x_sorted, [w_in, w_in],
        [pl.BlockSpec((None, K, tn1), lambda j, i, te, tf: (te[i], 0, j)),
         pl.BlockSpec((None, K, tn1), lambda j, i, te, tf: (te[i], 0, ncb + j))],
        [], [], [jax.ShapeDtypeStruct((P, F), BF16)],
        [pl.BlockSpec((tm, tn1), lambda j, i, te, tf: (i, j))],
        epi_in, grid=(ncb, P // tm), x_spec=pl.BlockSpec((tm, K), lambda j, i, te, tf: (i, 0)),
        w_block=(K, tn1), prologue=lambda v: v.astype(BF16), prefetch=(tile_expert, tile_first), recast=recast,
        name="moe_in")[0]

    def epi_out(accs, e_refs, o_refs, j, i, pf):
        o_refs[0][...] = accs[0]

    return fused_matmul(
        act, [w_out], [pl.BlockSpec((None, F, tn2), lambda j, i, te, tf: (te[i], 0, j))], [], [],
        [jax.ShapeDtypeStruct((P, K), F32)], [pl.BlockSpec((tm, tn2), lambda j, i, te, tf: (i, j))],
        epi_out, grid=(K // tn2, P // tm), x_spec=pl.BlockSpec((tm, F), lambda j, i, te, tf: (i, 0)),
        w_block=(F, tn2), prefetch=(tile_expert, tile_first), recast=recast, name="moe_out")[0]


def _combine_kernel(pa_ref, pb_ref, eo_hbm, h_ref, gt_ref, ga_ref, gb_ref, g_ref, out_ref, buf_a, buf_b, sem, *,
                    tok_offset):
    bs, tl, D = h_ref.shape
    tc = bs * tl
    base = tok_offset + pl.program_id(0) * tc

    def start(r, c):
        _row_copy(eo_hbm, pa_ref[base + r], buf_a, r, sem.at[0]).start()
        _row_copy(eo_hbm, pb_ref[base + r], buf_b, r, sem.at[1]).start()
        return c

    def wait(r, c):
        _row_copy(eo_hbm, 0, buf_a, r, sem.at[0]).wait()
        _row_copy(eo_hbm, 0, buf_b, r, sem.at[1]).wait()
        return c

    lax.fori_loop(0, tc, start, 0)
    lax.fori_loop(0, tc, wait, 0)
    y = ga_ref[...] * buf_a[...].reshape(bs, tl, D) + gb_ref[...] * buf_b[...].reshape(bs, tl, D)
    hn = h_ref[...] + gt_ref[...] * y
    out_ref[...] = hn * lax.rsqrt(jnp.mean(hn * hn, axis=-1, keepdims=True) + NORM_EPS) * g_ref[...]


def moe_combine_norm(eo, pos_a, pos_b, gate_a, gate_b, h, mod, gate_col, g, tok_offset, rows=128):
    n_seq, L, D = h.shape
    bs, tl = _seq_tiles(n_seq, L, rows)
    nl = L // tl
    blk = pl.BlockSpec((bs, tl, D), lambda i, pa, pb: (i // nl, i % nl, 0))
    wt = pl.BlockSpec((bs, tl, 1), lambda i, pa, pb: (i // nl, i % nl, 0))
    gs = pltpu.PrefetchScalarGridSpec(
        num_scalar_prefetch=2, grid=((n_seq * L) // (bs * tl),),
        in_specs=[pl.BlockSpec(memory_space=pl.ANY), blk,
                  pl.BlockSpec((bs, 1, D), lambda i, pa, pb: (i // nl, 0, gate_col)), wt, wt,
                  pl.BlockSpec((1, D), lambda i, pa, pb: (0, 0))],
        out_specs=blk,
        scratch_shapes=[pltpu.VMEM((bs * tl, D), F32), pltpu.VMEM((bs * tl, D), F32),
                        pltpu.SemaphoreType.DMA((2,))],
    )
    return pl.pallas_call(
        functools.partial(_combine_kernel, tok_offset=tok_offset), grid_spec=gs,
        out_shape=jax.ShapeDtypeStruct((n_seq, L, D), F32),
        compiler_params=_cparams(1), name="moe_combine_norm",
    )(pos_a, pos_b, eo, h, mod, gate_a, gate_b, g.reshape(1, D))


def _moe_plan(gate, sel, tm):
    T = gate.shape[0]
    P = TOP_K * T + N_EXPERTS * tm
    sel8 = sel[:, :N_EXPERTS]
    gate8 = gate[:, :N_EXPERTS]
    cnt = jnp.sum(sel8, axis=0)
    rank = jnp.cumsum(sel8, axis=0) - sel8
    cnt_pad = ((cnt + tm - 1) // tm) * tm
    ends = jnp.cumsum(cnt_pad)
    pos = (ends - cnt_pad)[None, :] + rank
    chosen = sel8 > 0
    pos_a = jnp.min(jnp.where(chosen, pos, P), axis=1).astype(jnp.int32)
    pos_b = jnp.max(jnp.where(chosen, pos, -1), axis=1).astype(jnp.int32)
    gate_a = jnp.sum(jnp.where(chosen & (pos == pos_a[:, None]), gate8, 0.0), axis=1)
    gate_b = jnp.sum(jnp.where(chosen & (pos == pos_b[:, None]), gate8, 0.0), axis=1)
    n_tiles = P // tm
    tile_start = jnp.arange(n_tiles, dtype=jnp.int32) * tm
    te = jnp.sum(tile_start[:, None] >= ends[None, :], axis=1).astype(jnp.int32)
    last_valid = jnp.maximum(ends[-1] // tm - 1, 0)
    te = jnp.where(tile_start < ends[-1], te, te[last_valid])
    te = jnp.minimum(te, N_EXPERTS - 1).astype(jnp.int32)
    tf = jnp.concatenate([jnp.ones((1,), jnp.int32), (te[1:] != te[:-1]).astype(jnp.int32)])
    return P, pos_a, pos_b, gate_a, gate_b, te, tf


def _layer0(h, mod, pos, w, caches):
    n_seq, L, D = h.shape
    cos, sin = _rope_tables(pos)
    n_g = len(A_GROUPS)
    xn = norm_mod(h, w['norm_mix_g'][0], mod, 0, 1)
    qkv = proj_rope(xn, w['a_w_qkv'][0], n_seq, L, cos, sin, lambda j: (j % 3) != 2, tn=A_COLS, name="a_qkv")
    if caches is None:
        outs, lses = zip(*[(band_attention_a if dil == 1 else dilated_attention_a)(qkv, g, dil)
                           for g, (_, dil) in enumerate(A_GROUPS)])
        h = merge_wo(outs, lses, w['a_w_o'][0], h, mod, 2)
        kv_new = [qkv[:, :, g * 3 * A_COLS + A_COLS:(g + 1) * 3 * A_COLS] for g in range(n_g)]
    else:
        tiles = qkv.reshape(n_seq, L, n_g, 3, A_HEADS, HEAD_DIM)
        merged = sample_attention_a(tiles.reshape(n_seq, L * n_g * 3 * A_HEADS, HEAD_DIM), caches)
        h = proj_residual(merged.reshape(n_seq * L, A_COLS).astype(BF16), w['a_w_o'][0], h, mod, 2, name="a_out")
        kv_new = [tiles[:, :, g, 1:].reshape(n_seq, L * 2 * A_HEADS, HEAD_DIM) for g in range(n_g)]
    xn = norm_mod(h, w['norm_ffn_g'][0], mod, 3, 4)
    act = swiglu_in(xn, w['ffn_w_in'][0], D_FF, name="ffn_in")
    h = proj_residual(act, w['ffn_w_out'][0], h, mod, 5, name="ffn_out")
    return h, kv_new


def _layer1_mixer(h, mod, kvmod, pos, w, past):
    n_seq, L, D = h.shape
    cos, sin = _rope_tables(pos)
    z = norm_mod(h, w['kv_norm_g'], kvmod, 0, 1)
    kvm = proj_rope(z, w['b_w_kv'], n_seq, L, cos, sin, lambda j: (j == 2) | (j == 4), tn=CMP_PLANE, name="b_kv")
    rows, win = kvm[:, :, :4 * CMP_PLANE], kvm[:, :, 4 * CMP_PLANE:]
    xn = norm_mod(h, w['norm_mix_g'][1], mod, 0, 1)
    n_q = B_HEADS * HEAD_DIM
    q = proj_rope(xn, w['b_w_qg'][0], n_seq, L, cos, sin, lambda j: j >= 0, tn=512, n_cols=n_q, name="b_q")
    wg = jnp.pad(w['b_w_qg'][0][:, n_q:], ((0, 0), (0, LANES - N_BRANCH * B_HEADS)))
    gates = proj_rope(xn, wg, n_seq, L, cos, sin, None, tn=LANES, sigmoid_out=True, name="b_gates")
    row_tiles = rows.reshape(n_seq, L, 4 * B_KV_HEADS, HEAD_DIM)
    if past is None:
        pages = row_tiles.reshape(n_seq * PAGES_PER_SEQ, PAGE_SIZE, 4 * B_KV_HEADS, HEAD_DIM)
        table = jnp.arange(n_seq * PAGES_PER_SEQ, dtype=jnp.int32).reshape(n_seq, PAGES_PER_SEQ)
        kc, vc = compress_pages(pages, table, w)
        gr = gates[:, :, :N_BRANCH * B_HEADS].reshape(n_seq, L, B_KV_HEADS, B_HPG * N_BRANCH).transpose(0, 2, 1, 3)
        o = nsa_prompt(q, gr, kc, vc, rows, win)
    else:
        pages, table, win_buf, past_len = past
        kc, vc = compress_pages(pages, table, w)
        o = nsa_sample(q, gates, kc, vc, pages, table, rows, win_buf, win, past_len)
    h = proj_residual(o.reshape(n_seq * L, n_q).astype(BF16), w['b_w_o'][0], h, mod, 2, name="b_out")
    return h, row_tiles, win


def kernel(x_prompt, x_sample, c_prompt, c_sample, cache_a_w128, cache_a_w512, cache_a_w2048, cache_b_kv,
           cache_b_win, page_table, norm_mix_g, norm_ffn_g, mod_w, mod_b, a_w_qkv, a_w_o, kv_norm_g, kv_mod_w,
           kv_mod_b, b_w_kv, cmp_w1_k, cmp_w2_k, cmp_pe_k, cmp_w1_v, cmp_w2_v, cmp_pe_v, b_w_qg, b_w_o,
           ffn_w_in, ffn_w_out, moe_w_router, moe_b_router, moe_w_in, moe_w_out, final_norm_g):
    w = {'norm_mix_g': norm_mix_g, 'norm_ffn_g': norm_ffn_g, 'a_w_qkv': a_w_qkv, 'a_w_o': a_w_o,
         'kv_norm_g': kv_norm_g, 'b_w_kv': b_w_kv, 'cmp_w1_k': cmp_w1_k, 'cmp_w2_k': cmp_w2_k,
         'cmp_pe_k': cmp_pe_k, 'cmp_w1_v': cmp_w1_v, 'cmp_w2_v': cmp_w2_v, 'cmp_pe_v': cmp_pe_v,
         'b_w_qg': b_w_qg, 'b_w_o': b_w_o, 'ffn_w_in': ffn_w_in, 'ffn_w_out': ffn_w_out}
    B, L, D = x_prompt.shape
    S, Q, _ = x_sample.shape
    past_len = page_table.shape[1] * PAGE_SIZE
    caches_a = (cache_a_w128, cache_a_w512, cache_a_w2048)

    n_c = B + S
    c_all = jnp.pad(jnp.concatenate([c_prompt, c_sample], axis=0), ((0, (-n_c) % SUBLANES), (0, 0)))
    mods = [mod_proj(c_all, mod_w[l], mod_b[l]) for l in range(2)]
    kvmod = mod_proj(c_all, kv_mod_w, kv_mod_b)

    def split(m):
        return m[:B].reshape(B, 1, -1), m[B:n_c].reshape(S, 1, -1)

    mod_p, mod_s = zip(*[split(m) for m in mods])
    kvmod_p, kvmod_s = split(kvmod)
    pos_p = jnp.arange(L, dtype=jnp.int32)
    pos_s = past_len + jnp.arange(Q, dtype=jnp.int32)

    h_p, kv_p = _layer0(x_prompt, mod_p[0], pos_p, w, None)
    h_s, kv_s = _layer0(x_sample, mod_s[0], pos_s, w, caches_a)
    a_out = []
    tok = 2 * A_HEADS
    for g, (window, _) in enumerate(A_GROUPS):
        keep_p = min(window, L)
        a_p = kv_p[g][:, L - keep_p:].reshape(1, B, keep_p, 2, A_HEADS, HEAD_DIM)
        n_buf = caches_a[g].shape[2]
        assert n_buf == window and window >= Q
        a_s = shift_append(caches_a[g].reshape(S, n_buf * tok, HEAD_DIM), kv_s[g])
        a_out += [a_p, a_s.reshape(1, S, n_buf, 2, A_HEADS, HEAD_DIM)]

    pages = cache_b_kv.reshape(cache_b_kv.shape[0], PAGE_SIZE, 4 * B_KV_HEADS, HEAD_DIM)
    n_win = cache_b_win.shape[1]
    win_tok = 2 * B_KV_HEADS
    win_buf = cache_b_win.reshape(S, n_win, win_tok, HEAD_DIM)
    h_p, rows_p, win_p = _layer1_mixer(h_p, mod_p[1], kvmod_p, pos_p, w, None)
    h_s, rows_s, win_s = _layer1_mixer(h_s, mod_s[1], kvmod_s, pos_s, w, (pages, page_table, win_buf, past_len))
    bkv_p = rows_p.reshape(B, L, 4, B_KV_HEADS, HEAD_DIM)
    bkv_s = rows_s.reshape(S, Q, 4, B_KV_HEADS, HEAD_DIM)
    keep = min(B_WINDOW, L)
    bwin_p = win_p[:, L - keep:].reshape(B, keep, 2, B_KV_HEADS, HEAD_DIM)
    assert n_win == B_WINDOW and B_WINDOW >= Q
    bwin_s = shift_append(cache_b_win.reshape(S, n_win * win_tok, HEAD_DIM), win_s.reshape(S, Q * win_tok, HEAD_DIM))
    bwin_s = bwin_s.reshape(S, n_win, 2, B_KV_HEADS, HEAD_DIM)

    xn_p = norm_mod(h_p, norm_ffn_g[1], mod_p[1], 3, 4, out_dtype=F32)
    xn_s = norm_mod(h_s, norm_ffn_g[1], mod_s[1], 3, 4, out_dtype=F32)
    routed = [moe_router(x, moe_w_router[0], moe_b_router[0]) for x in (xn_p, xn_s)]
    gate, sel = [jnp.concatenate(parts, axis=0) for parts in zip(*routed)]
    P, pos_a, pos_b, gate_a, gate_b, te, tf = _moe_plan(gate, sel, MOE_TILE)
    x_sorted = moe_dispatch(xn_p, pos_a, pos_b, jnp.zeros((P, D), F32), 0)
    x_sorted = moe_dispatch(xn_s, pos_a, pos_b, x_sorted, B * L)
    eo = moe_experts(x_sorted, moe_w_in[0], moe_w_out[0], te, tf)
    y_p = moe_combine_norm(eo, pos_a, pos_b, gate_a[:B * L].reshape(B, L, 1), gate_b[:B * L].reshape(B, L, 1),
                           h_p, mod_p[1], 5, final_norm_g, 0)
    y_s = moe_combine_norm(eo, pos_a, pos_b, gate_a[B * L:].reshape(S, Q, 1), gate_b[B * L:].reshape(S, Q, 1),
                           h_s, mod_s[1], 5, final_norm_g, B * L)

    return (y_p, y_s, a_out[0], a_out[1], a_out[2], a_out[3], a_out[4], a_out[5], bkv_p, bkv_s, bwin_p, bwin_s)
```

```python
import functools
import math

import numpy as np
import jax
import jax.numpy as jnp
from jax import lax
from jax.experimental import pallas as pl
from jax.experimental.pallas import tpu as pltpu

F32 = jnp.float32
BF16 = jnp.bfloat16

D_MODEL = 2048
HEAD_DIM = 128
ROPE_THETA = 10000.0
NORM_EPS = 1e-6
A_GROUPS = ((128, 1), (512, 4), (2048, 16))
A_HEADS = 8
A_COLS = A_HEADS * HEAD_DIM
B_HEADS = 16
B_KV_HEADS = 4
B_HPG = B_HEADS // B_KV_HEADS
N_BRANCH = 3
CMP_LEN = 32
CMP_STRIDE = 16
CMP_HIDDEN = 256
SLC_LEN = 64
N_SELECT = 16
B_WINDOW = 512
D_FF = 5632
N_EXPERTS = 8
TOP_K = 2
D_FF_EXPERT = 7168
PAGE_SIZE = 128
Q_BLOCK = 128
NEG_INF = -1e30
FORCE_SCORE = 1e9
LANES = 128
SUBLANES = 8
VMEM_LIMIT = 56 * 1024 * 1024
ATT_SCALE = HEAD_DIM ** -0.5


def _cparams(n_axes, vmem=VMEM_LIMIT):
    return pltpu.CompilerParams(dimension_semantics=("arbitrary",) * n_axes, vmem_limit_bytes=vmem)


def _silu(x):
    return x * jax.nn.sigmoid(x)


def _masked_softmax(s, mask):
    s = jnp.where(mask, s, NEG_INF)
    m = jnp.max(s, axis=-1, keepdims=True)
    e = jnp.where(mask, jnp.exp(s - m), 0.0)
    den = jnp.maximum(jnp.sum(e, axis=-1, keepdims=True), 1e-30)
    return e / den, m + jnp.log(den)


def _dot(a, b):
    return jnp.dot(a, b, preferred_element_type=F32)


def _dot_t(a, b):
    return lax.dot_general(a, b, (((1,), (1,)), ((), ())), preferred_element_type=F32)


def _rope_tables(pos):
    half = HEAD_DIM // 2
    inv = ROPE_THETA ** (-jnp.arange(half, dtype=F32) / half)
    ang = pos.astype(F32)[:, None] * inv[None, :]
    cos, sin = jnp.cos(ang), jnp.sin(ang)
    return jnp.concatenate([cos, cos], axis=-1), jnp.concatenate([-sin, sin], axis=-1)


def _rope_head(x, c, s):
    return x * c + pltpu.roll(x, HEAD_DIM // 2, axis=x.ndim - 1) * s


def _norm_mod_kernel(x_ref, g_ref, sh_ref, sc_ref, o_ref):
    x = x_ref[...]
    y = x * lax.rsqrt(jnp.mean(x * x, axis=-1, keepdims=True) + NORM_EPS) * g_ref[...]
    y = y * (1.0 + sc_ref[...]) + sh_ref[...]
    o_ref[...] = y.reshape(o_ref.shape).astype(o_ref.dtype)


def _seq_tiles(n_seq, seq_len, rows):
    if seq_len >= rows:
        return 1, rows
    return min(n_seq, rows // seq_len), seq_len


def norm_mod(h, g, mod, shift_col, scale_col, rows=256, out_dtype=BF16):
    n_seq, L, D = h.shape
    bs, tl = _seq_tiles(n_seq, L, rows)
    nl = L // tl
    return pl.pallas_call(
        _norm_mod_kernel,
        grid=(n_seq // bs, nl),
        in_specs=[
            pl.BlockSpec((bs, tl, D), lambda b, l: (b, l, 0)),
            pl.BlockSpec((1, D), lambda b, l: (0, 0)),
            pl.BlockSpec((bs, 1, D), lambda b, l: (b, 0, shift_col)),
            pl.BlockSpec((bs, 1, D), lambda b, l: (b, 0, scale_col)),
        ],
        out_specs=pl.BlockSpec((bs * tl, D), lambda b, l: (b * nl + l, 0)),
        out_shape=jax.ShapeDtypeStruct((n_seq * L, D), out_dtype),
        compiler_params=_cparams(2),
        name="norm_mod",
    )(h, g.reshape(1, D), mod, mod)


def fused_matmul(x, weights, w_specs, extras, extra_specs, out_shapes, out_specs, epilogue, *,
                 grid, x_spec, w_block, prologue=None, prefetch=(), recast=None, name="matmul"):
    nw, ne, no, npf = len(weights), len(extras), len(out_shapes), len(prefetch)

    def body(*refs):
        pf = refs[:npf]
        x_ref = refs[npf]
        w_refs = refs[npf + 1:npf + 1 + nw]
        e_refs = refs[npf + 1 + nw:npf + 1 + nw + ne]
        o_refs = refs[npf + 1 + nw + ne:npf + 1 + nw + ne + no]
        wb = refs[npf + 1 + nw + ne + no:]
        j, i = pl.program_id(0), pl.program_id(1)
        cond = (i == 0) if recast is None else recast(j, i, *pf)

        @pl.when(cond)
        def _():
            for k in range(nw):
                wb[k][...] = w_refs[k][...].astype(BF16)

        xv = x_ref[...]
        if prologue is not None:
            xv = prologue(xv)
        accs = [_dot(xv, wb[k][...]) for k in range(nw)]
        epilogue(accs, e_refs, o_refs, j, i, pf)

    gs = pltpu.PrefetchScalarGridSpec(
        num_scalar_prefetch=npf,
        grid=grid,
        in_specs=[x_spec] + list(w_specs) + list(extra_specs),
        out_specs=list(out_specs),
        scratch_shapes=[pltpu.VMEM(w_block, BF16) for _ in range(nw)],
    )
    return pl.pallas_call(body, grid_spec=gs, out_shape=list(out_shapes),
                          compiler_params=_cparams(2), name=name)(*prefetch, x, *weights, *extras)


def mod_proj(c, w, b, tn=1024):
    n, K = c.shape
    N = w.shape[-1]

    def epi(accs, e_refs, o_refs, j, i, pf):
        o_refs[0][...] = accs[0] + e_refs[0][...]

    return fused_matmul(
        c, [w], [pl.BlockSpec((K, tn), lambda j, i: (0, j))],
        [b.reshape(1, N)], [pl.BlockSpec((1, tn), lambda j, i: (0, j))],
        [jax.ShapeDtypeStruct((n, N), F32)], [pl.BlockSpec((n, tn), lambda j, i: (0, j))],
        epi, grid=(N // tn, 1), x_spec=pl.BlockSpec((n, K), lambda j, i: (0, 0)),
        w_block=(K, tn), prologue=lambda v: _silu(v).astype(BF16), name="mod_proj")[0]


def proj_rope(xn, w, n_seq, L, cos, sin, rope_pred, *, tn, rows=512, n_cols=None, sigmoid_out=False,
              name="proj"):
    K = xn.shape[1]
    N = w.shape[-1] if n_cols is None else n_cols
    bs, tl = _seq_tiles(n_seq, L, rows)
    nl = L // tl
    tm = bs * tl
    n_heads = tn // HEAD_DIM

    def epi(accs, e_refs, o_refs, j, i, pf):
        acc = accs[0]
        if rope_pred is None:
            if sigmoid_out:
                acc = jax.nn.sigmoid(acc)
            o_refs[0][...] = acc.reshape(bs, tl, tn)
            return
        do_rope = rope_pred(j)

        @pl.when(do_rope)
        def _():
            c = e_refs[0][...][None]
            s = e_refs[1][...][None]
            for h in range(n_heads):
                hs = slice(h * HEAD_DIM, (h + 1) * HEAD_DIM)
                a = acc[:, hs]
                r = pltpu.roll(a, HEAD_DIM // 2, axis=1)
                o_refs[0][:, :, hs] = a.reshape(bs, tl, HEAD_DIM) * c + r.reshape(bs, tl, HEAD_DIM) * s

        @pl.when(jnp.logical_not(do_rope))
        def _():
            o_refs[0][...] = acc.reshape(bs, tl, tn)

    w_spec = pl.BlockSpec((K, tn), lambda j, i: (0, j))
    tab_spec = pl.BlockSpec((tl, HEAD_DIM), lambda j, i: (i % nl, 0))
    return fused_matmul(
        xn, [w], [w_spec], [cos, sin], [tab_spec, tab_spec],
        [jax.ShapeDtypeStruct((n_seq, L, N), F32)],
        [pl.BlockSpec((bs, tl, tn), lambda j, i: (i // nl, i % nl, j))],
        epi, grid=(N // tn, (n_seq * L) // tm), x_spec=pl.BlockSpec((tm, K), lambda j, i: (i, 0)),
        w_block=(K, tn), name=name)[0]


def proj_residual(x, w, h, mod, gate_col, *, tn=512, rows=512, name="proj_res"):
    n_seq, L, N = h.shape
    K = x.shape[1]
    bs, tl = _seq_tiles(n_seq, L, rows)
    nl = L // tl
    tm = bs * tl
    ncb = N // tn

    def epi(accs, e_refs, o_refs, j, i, pf):
        o_refs[0][...] = e_refs[0][...] + e_refs[1][...] * accs[0].reshape(bs, tl, tn)

    blk = pl.BlockSpec((bs, tl, tn), lambda j, i: (i // nl, i % nl, j))
    return fused_matmul(
        x, [w], [pl.BlockSpec((K, tn), lambda j, i: (0, j))],
        [h, mod], [blk, pl.BlockSpec((bs, 1, tn), lambda j, i: (i // nl, 0, gate_col * ncb + j))],
        [jax.ShapeDtypeStruct((n_seq, L, N), F32)], [blk],
        epi, grid=(ncb, (n_seq * L) // tm), x_spec=pl.BlockSpec((tm, K), lambda j, i: (i, 0)),
        w_block=(K, tn), name=name)[0]


def swiglu_in(xn, w_in, d_ff, *, tn=512, tm=512, name="swiglu_in"):
    M, K = xn.shape
    ncb = d_ff // tn
    tm = min(tm, M)

    def epi(accs, e_refs, o_refs, j, i, pf):
        o_refs[0][...] = (_silu(accs[0]) * accs[1]).astype(BF16)

    return fused_matmul(
        xn, [w_in, w_in],
        [pl.BlockSpec((K, tn), lambda j, i: (0, j)), pl.BlockSpec((K, tn), lambda j, i: (0, ncb + j))],
        [], [], [jax.ShapeDtypeStruct((M, d_ff), BF16)], [pl.BlockSpec((tm, tn), lambda j, i: (i, j))],
        epi, grid=(ncb, M // tm), x_spec=pl.BlockSpec((tm, K), lambda j, i: (i, 0)),
        w_block=(K, tn), name=name)[0]


def _band_kernel(q_ref, kc_ref, kp_ref, vc_ref, vp_ref, o_ref, lse_ref, *, n_back):
    i = pl.program_id(2)
    qb = q_ref.shape[0]
    qi = lax.broadcasted_iota(jnp.int32, (qb, 2 * qb), 0)
    kj = lax.broadcasted_iota(jnp.int32, (qb, 2 * qb), 1)
    rel = qi + qb - kj
    mask = (rel >= 0) & (rel <= n_back) & ((kj >= qb) | (i > 0))
    lane = lax.broadcasted_iota(jnp.int32, (qb, A_HEADS), 1)
    lse_all = jnp.zeros((qb, A_HEADS), F32)
    for h in range(A_HEADS):
        hs = slice(h * HEAD_DIM, (h + 1) * HEAD_DIM)
        q = q_ref[:, hs].astype(BF16)
        k = jnp.concatenate([kp_ref[:, hs], kc_ref[:, hs]], axis=0).astype(BF16)
        v = jnp.concatenate([vp_ref[:, hs], vc_ref[:, hs]], axis=0).astype(BF16)
        s = _dot_t(q, k) * ATT_SCALE
        p, lse = _masked_softmax(s, mask)
        o_ref[:, hs] = _dot(p.astype(BF16), v)
        lse_all = jnp.where(lane == h, lse, lse_all)
    lse_ref[...] = lse_all


def band_attention_a(qkv, g, dil):
    B, L, NC = qkv.shape
    n_back = A_GROUPS[g][0] // dil
    assert n_back == Q_BLOCK
    Lr = L // dil
    qb = math.gcd(Lr, Q_BLOCK)
    nb = Lr // qb
    ncb = NC // A_COLS
    view = qkv.reshape(B, Lr, dil * NC)

    def col(part):
        return lambda b, r, i: (b, i, r * ncb + g * 3 + part)

    def col_prev(part):
        return lambda b, r, i: (b, jnp.maximum(i - 1, 0), r * ncb + g * 3 + part)

    blk = (None, qb, A_COLS)
    o, lse = pl.pallas_call(
        functools.partial(_band_kernel, n_back=n_back),
        grid=(B, dil, nb),
        in_specs=[pl.BlockSpec(blk, col(0)), pl.BlockSpec(blk, col(1)), pl.BlockSpec(blk, col_prev(1)),
                  pl.BlockSpec(blk, col(2)), pl.BlockSpec(blk, col_prev(2))],
        out_specs=[pl.BlockSpec(blk, lambda b, r, i: (b, i, r)),
                   pl.BlockSpec((None, None, qb, A_HEADS), lambda b, r, i: (b, r, i, 0))],
        out_shape=[jax.ShapeDtypeStruct((B, Lr, dil * A_COLS), F32),
                   jax.ShapeDtypeStruct((B, dil, Lr, A_HEADS), F32)],
        compiler_params=_cparams(3),
        name="band_attention_a",
    )(view, view, view, view, view)
    return o.reshape(B, L, A_COLS), lse.transpose(0, 2, 1, 3).reshape(B, L, A_HEADS)


def _dilated_kernel(*refs, dil, has_prev, heads, qc):
    if has_prev:
        q_ref, kc_ref, kp_ref, vc_ref, vp_ref, o_ref, lse_ref = refs
    else:
        q_ref, kc_ref, vc_ref, o_ref, lse_ref = refs
    i = pl.program_id(2)
    qb = q_ref.shape[0]
    span = dil * Q_BLOCK
    lane = lax.broadcasted_iota(jnp.int32, (qc, heads), 1)
    for c in range(qb // qc):
        n_prev = qb - c * qc if has_prev else 0
        n_keys = n_prev + (c + 1) * qc
        a = lax.broadcasted_iota(jnp.int32, (qc, n_keys), 0)
        x = lax.broadcasted_iota(jnp.int32, (qc, n_keys), 1)
        rel = a + (c * qc + n_prev) - x
        mask = (rel >= 0) & (rel <= span) & ((rel % dil) == 0)
        if has_prev:
            mask = mask & ((x >= n_prev) | (i > 0))
        lse_c = jnp.zeros((qc, heads), F32)
        for h in range(heads):
            hs = slice(h * HEAD_DIM, (h + 1) * HEAD_DIM)
            q = q_ref[c * qc:(c + 1) * qc, hs].astype(BF16)
            k = kc_ref[:(c + 1) * qc, hs]
            v = vc_ref[:(c + 1) * qc, hs]
            if has_prev:
                k = jnp.concatenate([kp_ref[c * qc:, hs], k], axis=0)
                v = jnp.concatenate([vp_ref[c * qc:, hs], v], axis=0)
            s = _dot_t(q, k.astype(BF16)) * ATT_SCALE
            p, lse = _masked_softmax(s, mask)
            o_ref[c * qc:(c + 1) * qc, hs] = _dot(p.astype(BF16), v.astype(BF16))
            lse_c = jnp.where(lane == h, lse, lse_c)
        lse_ref[c * qc:(c + 1) * qc, :] = lse_c


def dilated_attention_a(qkv, g, dil, qc=256):
    B, L, NC = qkv.shape
    assert A_GROUPS[g][0] // dil == Q_BLOCK
    qb = min(dil * Q_BLOCK, L)
    nb = L // qb
    has_prev = nb > 1
    heads = A_HEADS if has_prev else 2
    hc = heads * HEAD_DIM
    per = A_COLS // hc

    def col(part):
        return pl.BlockSpec((None, qb, hc), lambda b, hb, i: (b, i, (g * 3 + part) * per + hb))

    def col_prev(part):
        return pl.BlockSpec((None, qb, hc), lambda b, hb, i: (b, jnp.maximum(i - 1, 0), (g * 3 + part) * per + hb))

    if has_prev:
        in_specs = [col(0), col(1), col_prev(1), col(2), col_prev(2)]
    else:
        in_specs = [col(0), col(1), col(2)]
    o, lse = pl.pallas_call(
        functools.partial(_dilated_kernel, dil=dil, has_prev=has_prev, heads=heads, qc=qc),
        grid=(B, per, nb),
        in_specs=in_specs,
        out_specs=[pl.BlockSpec((None, qb, hc), lambda b, hb, i: (b, i, hb)),
                   pl.BlockSpec((None, None, qb, heads), lambda b, hb, i: (b, hb, i, 0))],
        out_shape=[jax.ShapeDtypeStruct((B, L, A_COLS), F32),
                   jax.ShapeDtypeStruct((B, per, L, heads), F32)],
        compiler_params=_cparams(3),
        name="dilated_attention_a",
    )(*([qkv] * len(in_specs)))
    return o, lse.transpose(0, 2, 1, 3).reshape(B, L, A_HEADS)


def _sample_a_kernel(qkv_ref, b0_ref, b1_ref, b2_ref, o_ref):
    bufs = (b0_ref, b1_ref, b2_ref)
    n_parts = 3 * A_HEADS
    tok_rows = len(A_GROUPS) * n_parts
    n_q = qkv_ref.shape[0] // tok_rows
    for qi in range(n_q):
        outs, lses = [], []
        for g, (window, dil) in enumerate(A_GROUPS):
            def new_tile(n, part):
                r0 = n * tok_rows + g * n_parts + part * A_HEADS
                return qkv_ref[r0:r0 + A_HEADS, :]

            q = new_tile(qi, 0)
            res, m0 = qi % dil, qi // dil
            r0 = res * 2 * A_HEADS
            new_toks = [n for n in range(qi + 1) if (qi - n) % dil == 0]
            k = jnp.concatenate([bufs[g][m0:, r0:r0 + A_HEADS, :]] + [new_tile(n, 1)[None] for n in new_toks], axis=0)
            v = jnp.concatenate([bufs[g][m0:, r0 + A_HEADS:r0 + 2 * A_HEADS, :]]
                                + [new_tile(n, 2)[None] for n in new_toks], axis=0)
            s = jnp.sum(k * q[None], axis=-1, keepdims=True) * ATT_SCALE
            m = jnp.max(s, axis=0, keepdims=True)
            e = jnp.exp(s - m)
            den = jnp.sum(e, axis=0, keepdims=True)
            outs.append(jnp.sum((e / den) * v, axis=0))
            lses.append((m + jnp.log(den))[0])
        mx = jnp.maximum(jnp.maximum(lses[0], lses[1]), lses[2])
        es = [jnp.exp(l - mx) for l in lses]
        den = es[0] + es[1] + es[2]
        o_ref[qi * A_HEADS:(qi + 1) * A_HEADS, :] = sum((e / den) * o for e, o in zip(es, outs))


def sample_attention_a(qkv_rows, caches):
    S, n_rows, _ = qkv_rows.shape
    n_q = n_rows // (len(A_GROUPS) * 3 * A_HEADS)
    views, specs = [], []
    for cache, (window, dil) in zip(caches, A_GROUPS):
        assert cache.shape[2] == window and window // dil == Q_BLOCK
        n_res = min(dil, n_q)
        views.append(cache.reshape(S, window // dil, dil * 2 * A_HEADS, HEAD_DIM))
        specs.append(pl.BlockSpec((None, window // dil, n_res * 2 * A_HEADS, HEAD_DIM), lambda s: (s, 0, 0, 0)))
    return pl.pallas_call(
        _sample_a_kernel,
        grid=(S,),
        in_specs=[pl.BlockSpec((None, n_rows, HEAD_DIM), lambda s: (s, 0, 0))] + specs,
        out_specs=pl.BlockSpec((None, n_q * A_HEADS, HEAD_DIM), lambda s: (s, 0, 0)),
        out_shape=jax.ShapeDtypeStruct((S, n_q * A_HEADS, HEAD_DIM), F32),
        compiler_params=_cparams(1),
        name="sample_attention_a",
    )(qkv_rows, *views)


def _shift_append_kernel(main_ref, tail_ref, new_ref, out_ref, *, nb):
    i = pl.program_id(1)
    rows = main_ref.shape[0]
    r = new_ref.shape[0]
    out_ref[:rows - r, :] = main_ref[r:, :]

    @pl.when(i < nb - 1)
    def _():
        out_ref[rows - r:, :] = tail_ref[...]

    @pl.when(i == nb - 1)
    def _():
        out_ref[rows - r:, :] = new_ref[...]


SHIFT_BLOCK_ROWS = 4096


def shift_append(buf, new):
    S, R, C = buf.shape
    r = new.shape[1]
    rows = min(SHIFT_BLOCK_ROWS, R)
    assert R % rows == 0 and rows % r == 0 and r % SUBLANES == 0
    nb = R // rows
    per = rows // r
    return pl.pallas_call(
        functools.partial(_shift_append_kernel, nb=nb),
        grid=(S, nb),
        in_specs=[pl.BlockSpec((None, rows, C), lambda s, i: (s, i, 0)),
                  pl.BlockSpec((None, r, C), lambda s, i: (s, jnp.minimum(i + 1, nb - 1) * per, 0)),
                  pl.BlockSpec((None, r, C), lambda s, i: (s, 0, 0))],
        out_specs=pl.BlockSpec((None, rows, C), lambda s, i: (s, i, 0)),
        out_shape=jax.ShapeDtypeStruct(buf.shape, buf.dtype),
        compiler_params=_cparams(2),
        name="shift_append",
    )(buf, buf, new)


def _merge_wo_kernel(o0_ref, o1_ref, o2_ref, l0_ref, l1_ref, l2_ref, w_ref, h_ref, gt_ref, out_ref, wb):
    @pl.when(pl.program_id(0) == 0)
    def _():
        wb[...] = w_ref[...].astype(BF16)

    bs, tl, N = h_ref.shape
    ls = [l0_ref[...], l1_ref[...], l2_ref[...]]
    mx = jnp.maximum(jnp.maximum(ls[0], ls[1]), ls[2])
    es = [jnp.exp(l - mx) for l in ls]
    den = es[0] + es[1] + es[2]
    ws = [e / den for e in es]
    o_refs = (o0_ref, o1_ref, o2_ref)
    parts = []
    for h in range(A_HEADS):
        hs = slice(h * HEAD_DIM, (h + 1) * HEAD_DIM)
        acc = ws[0][:, h:h + 1] * o_refs[0][:, hs]
        for g in (1, 2):
            acc = acc + ws[g][:, h:h + 1] * o_refs[g][:, hs]
        parts.append(acc.astype(BF16))
    merged = jnp.concatenate(parts, axis=1)
    y = _dot(merged, wb[...])
    out_ref[...] = h_ref[...] + gt_ref[...] * y.reshape(bs, tl, N)


def merge_wo(os_, lses, w_o, h, mod, gate_col, rows=256):
    n_seq, L, N = h.shape
    bs, tl = _seq_tiles(n_seq, L, rows)
    nl = L // tl
    tm = bs * tl
    K = w_o.shape[0]
    o_spec = pl.BlockSpec((tm, K), lambda i: (i, 0))
    l_spec = pl.BlockSpec((tm, A_HEADS), lambda i: (i, 0))
    blk = pl.BlockSpec((bs, tl, N), lambda i: (i // nl, i % nl, 0))
    return pl.pallas_call(
        _merge_wo_kernel,
        grid=((n_seq * L) // tm,),
        in_specs=[o_spec] * 3 + [l_spec] * 3 + [
            pl.BlockSpec((K, N), lambda i: (0, 0)), blk,
            pl.BlockSpec((bs, 1, N), lambda i: (i // nl, 0, gate_col))],
        out_specs=blk,
        out_shape=jax.ShapeDtypeStruct((n_seq, L, N), F32),
        scratch_shapes=[pltpu.VMEM((K, N), BF16)],
        compiler_params=_cparams(1),
        name="merge_wo",
    )(*[o.reshape(n_seq * L, K) for o in os_], *[l.reshape(n_seq * L, A_HEADS) for l in lses], w_o, h, mod)


PAGES_PER_SEQ = 16
CHUNKS_PER_PAGE = PAGE_SIZE // CMP_STRIDE
N_CHUNK = PAGES_PER_SEQ * CHUNKS_PER_PAGE
CMP_PLANE = B_KV_HEADS * HEAD_DIM
HALF_FLAT = CMP_STRIDE * HEAD_DIM


def _compress_kernel(pt_ref, *refs):
    pages = refs[:PAGES_PER_SEQ]
    (w1k_ref, w2k_ref, pek_ref, w1v_ref, w2v_ref, pev_ref, cos_ref, sin_ref,
     kc_ref, vc_ref, w1k_b, w1v_b, w2k_b, w2v_b, pe_hid) = refs[PAGES_PER_SEQ:]

    @pl.when(pl.program_id(0) == 0)
    def _():
        w1k_b[...] = w1k_ref[...].astype(BF16)
        w1v_b[...] = w1v_ref[...].astype(BF16)
        w2k_b[...] = w2k_ref[...].astype(BF16)
        w2v_b[...] = w2v_ref[...].astype(BF16)
        for plane, (pe_ref, w1_b) in enumerate(((pek_ref, w1k_b), (pev_ref, w1v_b))):
            flat = jnp.concatenate([pe_ref[r:r + 1, :] for r in range(CMP_LEN)], axis=1)
            flat = jnp.broadcast_to(flat, (SUBLANES, CMP_LEN * HEAD_DIM)).astype(BF16)
            pe_hid[plane] = _dot(flat, w1_b[...])

    pair = 2 * CMP_STRIDE
    n_pairs = CHUNKS_PER_PAGE // 2
    for plane, (w1_b, w2_b, out_ref) in enumerate(((w1k_b, w2k_b, kc_ref), (w1v_b, w2v_b, vc_ref))):
        js = slice(plane * B_KV_HEADS, (plane + 1) * B_KV_HEADS)
        cols = []
        for r in range(CMP_STRIDE):
            tiles = []
            for pg in pages:
                even = pg[pl.ds(r, n_pairs, stride=pair), js, :]
                odd = pg[pl.ds(r + CMP_STRIDE, n_pairs, stride=pair), js, :]
                tiles.append(jnp.concatenate([even, odd], axis=1))
            cols.append(jnp.concatenate(tiles, axis=0).reshape(N_CHUNK * B_KV_HEADS, HEAD_DIM).astype(BF16))
        lhs = jnp.concatenate(cols, axis=1)
        first = _dot(lhs, w1_b[:HALF_FLAT, :])
        second = _dot(lhs, w1_b[HALF_FLAT:, :])
        hid = first + pltpu.roll(second, (N_CHUNK - 1) * B_KV_HEADS, axis=0) + pe_hid[plane][0:1, :]
        out = _dot(_silu(hid).astype(BF16), w2_b[...])
        if plane == 0:
            out = _rope_head(out, cos_ref[...], sin_ref[...])
        out_ref[...] = out


def compress_pages(pages, page_table, w):
    n_seq = page_table.shape[0]
    pos = jnp.arange(N_CHUNK, dtype=jnp.int32) * CMP_STRIDE + CMP_LEN - 1
    cos, sin = _rope_tables(jnp.repeat(pos, B_KV_HEADS))
    n_rows = N_CHUNK * B_KV_HEADS

    def page_spec(p):
        return pl.BlockSpec((None, PAGE_SIZE, 2 * B_KV_HEADS, HEAD_DIM), lambda s, pt: (pt[s, p], 0, 0, 0))

    def const(shape):
        return pl.BlockSpec(shape, lambda s, pt: (0,) * len(shape))

    flat_rows = CMP_LEN * HEAD_DIM
    gs = pltpu.PrefetchScalarGridSpec(
        num_scalar_prefetch=1,
        grid=(n_seq,),
        in_specs=[page_spec(p) for p in range(PAGES_PER_SEQ)] + [
            const((flat_rows, CMP_HIDDEN)), const((CMP_HIDDEN, HEAD_DIM)), const((CMP_LEN, HEAD_DIM)),
            const((flat_rows, CMP_HIDDEN)), const((CMP_HIDDEN, HEAD_DIM)), const((CMP_LEN, HEAD_DIM)),
            const((n_rows, HEAD_DIM)), const((n_rows, HEAD_DIM))],
        out_specs=[pl.BlockSpec((None, n_rows, HEAD_DIM), lambda s, pt: (s, 0, 0))] * 2,
        scratch_shapes=[pltpu.VMEM((flat_rows, CMP_HIDDEN), BF16), pltpu.VMEM((flat_rows, CMP_HIDDEN), BF16),
                        pltpu.VMEM((CMP_HIDDEN, HEAD_DIM), BF16), pltpu.VMEM((CMP_HIDDEN, HEAD_DIM), BF16),
                        pltpu.VMEM((2, SUBLANES, CMP_HIDDEN), F32)],
    )
    return pl.pallas_call(
        _compress_kernel, grid_spec=gs,
        out_shape=[jax.ShapeDtypeStruct((n_seq, n_rows, HEAD_DIM), F32)] * 2,
        compiler_params=_cparams(1), name="compress_pages",
    )(page_table, *([pages] * PAGES_PER_SEQ), w['cmp_w1_k'], w['cmp_w2_k'], w['cmp_pe_k'],
      w['cmp_w1_v'], w['cmp_w2_v'], w['cmp_pe_v'], cos, sin)


def _overlap_matrix(n_slc):
    c0 = np.arange(LANES) * CMP_STRIDE
    s0 = np.arange(LANES) * SLC_LEN
    m = (c0[:, None] < s0[None, :] + SLC_LEN) & (c0[:, None] + CMP_LEN > s0[None, :])
    m &= (np.arange(LANES)[:, None] < N_CHUNK - 1) & (np.arange(LANES)[None, :] < n_slc)
    return jnp.asarray(m.astype(np.float32), dtype=BF16)


def _expand_matrix(n_keys):
    m = (np.arange(n_keys)[None, :] // SLC_LEN) == np.arange(LANES)[:, None]
    return jnp.asarray(m.astype(np.float32), dtype=BF16)


def _select_blocks(p_slc, q_pos, n_slc):
    nq = p_slc.shape[0]
    blk = lax.broadcasted_iota(jnp.int32, (nq, LANES), 1)
    cur = q_pos // SLC_LEN
    forced = (blk == 0) | (blk == cur) | (blk == cur - 1)
    imp = jnp.where(forced, FORCE_SCORE, p_slc)
    imp = jnp.where(blk <= cur, imp, NEG_INF)
    imp = jnp.where(blk < n_slc, imp, -3e38)
    rank = jnp.zeros((nq, LANES), jnp.int32)
    for k in range(n_slc):
        vk = imp[:, k:k + 1]
        ahead = (vk > imp) | ((vk == imp) & (blk > k))
        rank = rank + ahead.astype(jnp.int32)
    sel = (rank < min(N_SELECT, n_slc)) & (blk < n_slc)
    return jnp.where(sel, 1.0, 0.0).astype(BF16)


def _cmp_branch(q, kc, vc, q_pos, overlap, n_q):
    s = _dot_t(q, kc.astype(BF16)) * ATT_SCALE
    n = lax.broadcasted_iota(jnp.int32, (n_q, N_CHUNK), 1)
    mask = (n * CMP_STRIDE + CMP_LEN - 1 <= q_pos) & (n < N_CHUNK - 1)
    p, _ = _masked_softmax(s.reshape(B_HPG, n_q, N_CHUNK), mask[None])
    o = _dot(p.reshape(B_HPG * n_q, N_CHUNK).astype(BF16), vc.astype(BF16))
    p_grp = jnp.sum(p, axis=0)
    return o, _dot(p_grp.astype(BF16), overlap)


def _nsa_prompt_kernel(q_ref, g_ref, kc_ref, vc_ref, ks_ref, vs_ref, kw_ref, vw_ref, ov_ref, ex_ref,
                       o_ref, ks_b, vs_b, kw_b, vw_b):
    i = pl.program_id(2)
    L = ks_ref.shape[0]
    nq = q_ref.shape[0]
    n_slc = L // SLC_LEN

    @pl.when(i == 0)
    def _():
        ks_b[...] = ks_ref[...].astype(BF16)
        vs_b[...] = vs_ref[...].astype(BF16)
        kw_b[...] = kw_ref[...].astype(BF16)
        vw_b[...] = vw_ref[...].astype(BF16)

    q = jnp.concatenate([q_ref[:, j * HEAD_DIM:(j + 1) * HEAD_DIM] for j in range(B_HPG)], axis=0).astype(BF16)
    q_pos = i * nq + lax.broadcasted_iota(jnp.int32, (nq, 1), 0)

    grp = pl.ds(pl.program_id(1), N_CHUNK, stride=B_KV_HEADS)
    o_cmp, p_slc = _cmp_branch(q, kc_ref[grp, :], vc_ref[grp, :], q_pos, ov_ref[...], nq)
    sel = _select_blocks(p_slc, q_pos, n_slc)
    sel_keys = _dot(sel, ex_ref[...])

    k_pos = lax.broadcasted_iota(jnp.int32, (nq, L), 1)
    mask = (sel_keys > 0.5) & (k_pos <= q_pos)
    s = _dot_t(q, ks_b[...]) * ATT_SCALE
    p, _ = _masked_softmax(s.reshape(B_HPG, nq, L), mask[None])
    o_slc = _dot(p.reshape(B_HPG * nq, L).astype(BF16), vs_b[...])

    n_win = B_WINDOW + nq
    start = pl.multiple_of(jnp.maximum(i * nq - B_WINDOW, 0), nq)
    kw = kw_b[pl.ds(start, n_win), :]
    vw = vw_b[pl.ds(start, n_win), :]
    rel = q_pos - (start + lax.broadcasted_iota(jnp.int32, (nq, n_win), 1))
    mask_w = (rel >= 0) & (rel <= B_WINDOW)
    s = _dot_t(q, kw) * ATT_SCALE
    p, _ = _masked_softmax(s.reshape(B_HPG, nq, n_win), mask_w[None])
    o_win = _dot(p.reshape(B_HPG * nq, n_win).astype(BF16), vw)

    gates = g_ref[...]
    for j in range(B_HPG):
        rows = slice(j * nq, (j + 1) * nq)
        c = j * N_BRANCH
        o_ref[:, j * HEAD_DIM:(j + 1) * HEAD_DIM] = (
            gates[:, c:c + 1] * o_cmp[rows] + gates[:, c + 1:c + 2] * o_slc[rows] + gates[:, c + 2:c + 3] * o_win[rows])


def nsa_prompt(q, gates, kc, vc, rows, win):
    B, L, _ = q.shape
    nq = Q_BLOCK
    assert L == N_CHUNK * CMP_STRIDE and L % SLC_LEN == 0
    gw = B_HPG * HEAD_DIM
    full = (None, L, HEAD_DIM)
    return pl.pallas_call(
        _nsa_prompt_kernel,
        grid=(B, B_KV_HEADS, L // nq),
        in_specs=[
            pl.BlockSpec((None, nq, gw), lambda b, g, i: (b, i, g)),
            pl.BlockSpec((None, None, nq, B_HPG * N_BRANCH), lambda b, g, i: (b, g, i, 0)),
            pl.BlockSpec((None, N_CHUNK * B_KV_HEADS, HEAD_DIM), lambda b, g, i: (b, 0, 0)),
            pl.BlockSpec((None, N_CHUNK * B_KV_HEADS, HEAD_DIM), lambda b, g, i: (b, 0, 0)),
            pl.BlockSpec(full, lambda b, g, i: (b, 0, 2 * B_KV_HEADS + g)),
            pl.BlockSpec(full, lambda b, g, i: (b, 0, 3 * B_KV_HEADS + g)),
            pl.BlockSpec(full, lambda b, g, i: (b, 0, g)),
            pl.BlockSpec(full, lambda b, g, i: (b, 0, B_KV_HEADS + g)),
            pl.BlockSpec((LANES, LANES), lambda b, g, i: (0, 0)),
            pl.BlockSpec((LANES, L), lambda b, g, i: (0, 0)),
        ],
        out_specs=pl.BlockSpec((None, nq, gw), lambda b, g, i: (b, i, g)),
        out_shape=jax.ShapeDtypeStruct((B, L, B_HEADS * HEAD_DIM), F32),
        scratch_shapes=[pltpu.VMEM((L, HEAD_DIM), BF16) for _ in range(4)],
        compiler_params=_cparams(3),
        name="nsa_prompt",
    )(q, gates, kc, vc, rows, rows, win, win, _overlap_matrix(L // SLC_LEN), _expand_matrix(L))


def _nsa_sample_kernel(pt_ref, *refs, past_len):
    pages = refs[:PAGES_PER_SEQ]
    (q_ref, g_ref, kc_ref, vc_ref, new_ref, wbuf_ref, wnew_ref, ov_ref, ex_ref, o_ref,
     stage, wstage) = refs[PAGES_PER_SEQ:]
    nq = q_ref.shape[0]
    tok = 2 * B_KV_HEADS
    for p, pg in enumerate(pages):
        stage[p * PAGE_SIZE * tok:(p + 1) * PAGE_SIZE * tok, :] = pg[...].reshape(PAGE_SIZE * tok, HEAD_DIM)
    wstage[...] = wbuf_ref[...].reshape(wstage.shape)
    n_slc = -(-(past_len + nq) // SLC_LEN)
    q_pos = past_len + lax.broadcasted_iota(jnp.int32, (nq, 1), 0)
    qn = lax.broadcasted_iota(jnp.int32, (nq, nq), 0)
    nn = lax.broadcasted_iota(jnp.int32, (nq, nq), 1)
    causal_new = nn <= qn
    n_buf = wbuf_ref.shape[0]
    mb = lax.broadcasted_iota(jnp.int32, (nq, n_buf), 1)
    rel_b = q_pos - (past_len - n_buf + mb)
    mask_wb = (rel_b >= 0) & (rel_b <= B_WINDOW)
    rel_n = qn - nn
    mask_wn = (rel_n >= 0) & (rel_n <= B_WINDOW)
    gates = g_ref[...]

    for g in range(B_KV_HEADS):
        gs = slice(g * HEAD_DIM, (g + 1) * HEAD_DIM)
        vs_ = slice(CMP_PLANE + g * HEAD_DIM, CMP_PLANE + (g + 1) * HEAD_DIM)
        q = jnp.concatenate(
            [q_ref[:, (g * B_HPG + j) * HEAD_DIM:(g * B_HPG + j + 1) * HEAD_DIM] for j in range(B_HPG)],
            axis=0).astype(BF16)
        grp = pl.ds(g, N_CHUNK, stride=B_KV_HEADS)
        o_cmp, p_slc = _cmp_branch(q, kc_ref[grp, :], vc_ref[grp, :], q_pos, ov_ref[...], nq)
        sel = _select_blocks(p_slc, q_pos, n_slc)
        sel_keys = _dot(sel, ex_ref[...])

        k_past = stage[pl.ds(g, past_len, stride=tok), :].astype(BF16)
        v_past = stage[pl.ds(B_KV_HEADS + g, past_len, stride=tok), :].astype(BF16)
        k_new = new_ref[:, 2 * CMP_PLANE + g * HEAD_DIM:2 * CMP_PLANE + (g + 1) * HEAD_DIM].astype(BF16)
        v_new = new_ref[:, 3 * CMP_PLANE + g * HEAD_DIM:3 * CMP_PLANE + (g + 1) * HEAD_DIM].astype(BF16)
        mask_p = sel_keys[:, :past_len] > 0.5
        mask_n = (sel_keys[:, past_len:past_len + nq] > 0.5) & causal_new
        o_slc = _two_part_attention(q, k_past, v_past, mask_p, k_new, v_new, mask_n, nq)

        kw = wstage[pl.ds(g, n_buf, stride=tok), :].astype(BF16)
        vw = wstage[pl.ds(B_KV_HEADS + g, n_buf, stride=tok), :].astype(BF16)
        kwn = wnew_ref[:, gs].astype(BF16)
        vwn = wnew_ref[:, vs_].astype(BF16)
        o_win = _two_part_attention(q, kw, vw, mask_wb, kwn, vwn, mask_wn, nq)

        for j in range(B_HPG):
            rows = slice(j * nq, (j + 1) * nq)
            hd = g * B_HPG + j
            c = hd * N_BRANCH
            o_ref[:, hd * HEAD_DIM:(hd + 1) * HEAD_DIM] = (
                gates[:, c:c + 1] * o_cmp[rows] + gates[:, c + 1:c + 2] * o_slc[rows]
                + gates[:, c + 2:c + 3] * o_win[rows])


def _two_part_attention(q, k1, v1, mask1, k2, v2, mask2, nq):
    n1, n2 = k1.shape[0], k2.shape[0]
    s1 = jnp.where(mask1[None], (_dot_t(q, k1) * ATT_SCALE).reshape(B_HPG, nq, n1), NEG_INF)
    s2 = jnp.where(mask2[None], (_dot_t(q, k2) * ATT_SCALE).reshape(B_HPG, nq, n2), NEG_INF)
    m = jnp.maximum(jnp.max(s1, axis=-1, keepdims=True), jnp.max(s2, axis=-1, keepdims=True))
    e1 = jnp.where(mask1[None], jnp.exp(s1 - m), 0.0)
    e2 = jnp.where(mask2[None], jnp.exp(s2 - m), 0.0)
    den = jnp.maximum(jnp.sum(e1, axis=-1, keepdims=True) + jnp.sum(e2, axis=-1, keepdims=True), 1e-30)
    o = _dot((e1 / den).reshape(B_HPG * nq, n1).astype(BF16), v1)
    return o + _dot((e2 / den).reshape(B_HPG * nq, n2).astype(BF16), v2)


def nsa_sample(q, gates, kc, vc, pages, page_table, new_rows, win_buf, win_new, past_len):
    S, nq, _ = q.shape
    assert past_len == PAGES_PER_SEQ * PAGE_SIZE and nq <= SLC_LEN
    n_keys = past_len + LANES

    def page_spec(p):
        return pl.BlockSpec((None, PAGE_SIZE, 2 * B_KV_HEADS, HEAD_DIM), lambda s, pt: (pt[s, p], 0, 1, 0))

    def per_seq(rows, cols):
        return pl.BlockSpec((None, rows, cols), lambda s, pt: (s, 0, 0))

    gs = pltpu.PrefetchScalarGridSpec(
        num_scalar_prefetch=1,
        grid=(S,),
        in_specs=[page_spec(p) for p in range(PAGES_PER_SEQ)] + [
            per_seq(nq, B_HEADS * HEAD_DIM), per_seq(nq, LANES), per_seq(N_CHUNK * B_KV_HEADS, HEAD_DIM),
            per_seq(N_CHUNK * B_KV_HEADS, HEAD_DIM), per_seq(nq, 4 * CMP_PLANE),
            pl.BlockSpec((None, win_buf.shape[1], 2 * B_KV_HEADS, HEAD_DIM), lambda s, pt: (s, 0, 0, 0)),
            per_seq(nq, 2 * CMP_PLANE),
            pl.BlockSpec((LANES, LANES), lambda s, pt: (0, 0)),
            pl.BlockSpec((LANES, n_keys), lambda s, pt: (0, 0))],
        out_specs=per_seq(nq, B_HEADS * HEAD_DIM),
        scratch_shapes=[pltpu.VMEM((past_len * 2 * B_KV_HEADS, HEAD_DIM), F32),
                        pltpu.VMEM((win_buf.shape[1] * 2 * B_KV_HEADS, HEAD_DIM), F32)],
    )
    n_slc = -(-(past_len + nq) // SLC_LEN)
    return pl.pallas_call(
        functools.partial(_nsa_sample_kernel, past_len=past_len), grid_spec=gs,
        out_shape=jax.ShapeDtypeStruct((S, nq, B_HEADS * HEAD_DIM), F32),
        compiler_params=_cparams(1), name="nsa_sample",
    )(page_table, *([pages] * PAGES_PER_SEQ), q, gates, kc, vc, new_rows, win_buf, win_new,
      _overlap_matrix(n_slc), _expand_matrix(n_keys))


MOE_TILE = 256


def _router_kernel(x_ref, w_ref, b_ref, gate_ref, sel_ref):
    logits = _dot(x_ref[...].astype(BF16), w_ref[...].astype(BF16)) + b_ref[...]
    lane = lax.broadcasted_iota(jnp.int32, logits.shape, 1)
    logits = jnp.where(lane < N_EXPERTS, logits, -jnp.inf)
    m1 = jnp.max(logits, axis=-1, keepdims=True)
    i1 = jnp.min(jnp.where(logits == m1, lane, LANES), axis=-1, keepdims=True)
    rest = jnp.where(lane == i1, -jnp.inf, logits)
    m2 = jnp.max(rest, axis=-1, keepdims=True)
    i2 = jnp.min(jnp.where(rest == m2, lane, LANES), axis=-1, keepdims=True)
    e2 = jnp.exp(m2 - m1)
    den = 1.0 + e2
    gate_ref[...] = jnp.where(lane == i1, 1.0 / den, jnp.where(lane == i2, e2 / den, 0.0))
    sel_ref[...] = ((lane == i1) | (lane == i2)).astype(jnp.int32)


def moe_router(xn, w_router, b_router, tm=512):
    T, D = xn.shape
    wp = jnp.pad(w_router, ((0, 0), (0, LANES - N_EXPERTS)))
    bp = jnp.pad(b_router, (0, LANES - N_EXPERTS)).reshape(1, LANES)
    return pl.pallas_call(
        _router_kernel,
        grid=(T // tm,),
        in_specs=[pl.BlockSpec((tm, D), lambda i: (i, 0)), pl.BlockSpec((D, LANES), lambda i: (0, 0)),
                  pl.BlockSpec((1, LANES), lambda i: (0, 0))],
        out_specs=[pl.BlockSpec((tm, LANES), lambda i: (i, 0))] * 2,
        out_shape=[jax.ShapeDtypeStruct((T, LANES), F32), jax.ShapeDtypeStruct((T, LANES), jnp.int32)],
        compiler_params=_cparams(1), name="moe_router",
    )(xn, wp, bp)


def _row_copy(src, src_row, dst, dst_row, sem):
    return pltpu.make_async_copy(src.at[pl.ds(src_row, 1), :], dst.at[pl.ds(dst_row, 1), :], sem)


def _dispatch_kernel(pa_ref, pb_ref, x_ref, prev_hbm, out_hbm, sem, *, tok_offset):
    del prev_hbm
    tc = x_ref.shape[0]
    base = tok_offset + pl.program_id(0) * tc

    def start(r, c):
        _row_copy(x_ref, r, out_hbm, pa_ref[base + r], sem.at[0]).start()
        _row_copy(x_ref, r, out_hbm, pb_ref[base + r], sem.at[1]).start()
        return c

    def wait(r, c):
        _row_copy(x_ref, r, out_hbm, 0, sem.at[0]).wait()
        _row_copy(x_ref, r, out_hbm, 0, sem.at[1]).wait()
        return c

    lax.fori_loop(0, tc, start, 0)
    lax.fori_loop(0, tc, wait, 0)


def moe_dispatch(x, pos_a, pos_b, buf, tok_offset, tc=256):
    T, D = x.shape
    gs = pltpu.PrefetchScalarGridSpec(
        num_scalar_prefetch=2, grid=(T // tc,),
        in_specs=[pl.BlockSpec((tc, D), lambda i, pa, pb: (i, 0)), pl.BlockSpec(memory_space=pl.ANY)],
        out_specs=pl.BlockSpec(memory_space=pl.ANY),
        scratch_shapes=[pltpu.SemaphoreType.DMA((2,))],
    )
    return pl.pallas_call(
        functools.partial(_dispatch_kernel, tok_offset=tok_offset), grid_spec=gs,
        out_shape=jax.ShapeDtypeStruct(buf.shape, buf.dtype), input_output_aliases={3: 0},
        compiler_params=_cparams(1), name="moe_dispatch")(pos_a, pos_b, x, buf)


def moe_experts(x_sorted, w_in, w_out, tile_expert, tile_first, tm=MOE_TILE):
    P, K = x_sorted.shape
    F = w_out.shape[1]
    tn1, tn2 = 1024, 512
    ncb = F // tn1

    def recast(j, i, te, tf):
        return tf[i] == 1

    def epi_in(accs, e_refs, o_refs, j, i, pf):
        o_refs[0][...] = (_silu(accs[0]) * accs[1]).astype(BF16)

    act = fused_matmul(
        x_sorted, [w_in, w_in],
        [pl.BlockSpec((None, K, tn1), lambda j, i, te, tf: (te[i], 0, j)),
         pl.BlockSpec((None, K, tn1), lambda j, i, te, tf: (te[i], 0, ncb + j))],
        [], [], [jax.ShapeDtypeStruct((P, F), BF16)],
        [pl.BlockSpec((tm, tn1), lambda j, i, te, tf: (i, j))],
        epi_in, grid=(ncb, P // tm), x_spec=pl.BlockSpec((tm, K), lambda j, i, te, tf: (i, 0)),
        w_block=(K, tn1), prologue=lambda v: v.astype(BF16), prefetch=(tile_expert, tile_first), recast=recast,
        name="moe_in")[0]

    def epi_out(accs, e_refs, o_refs, j, i, pf):
        o_refs[0][...] = accs[0]

    return fused_matmul(
        act, [w_out], [pl.BlockSpec((None, F, tn2), lambda j, i, te, tf: (te[i], 0, j))], [], [],
        [jax.ShapeDtypeStruct((P, K), F32)], [pl.BlockSpec((tm, tn2), lambda j, i, te, tf: (i, j))],
        epi_out, grid=(K // tn2, P // tm), x_spec=pl.BlockSpec((tm, F), lambda j, i, te, tf: (i, 0)),
        w_block=(F, tn2), prefetch=(tile_expert, tile_first), recast=recast, name="moe_out")[0]


def _combine_kernel(pa_ref, pb_ref, eo_hbm, h_ref, gt_ref, ga_ref, gb_ref, g_ref, out_ref, buf_a, buf_b, sem, *,
                    tok_offset):
    bs, tl, D = h_ref.shape
    tc = bs * tl
    base = tok_offset + pl.program_id(0) * tc

    def start(r, c):
        _row_copy(eo_hbm, pa_ref[base + r], buf_a, r, sem.at[0]).start()
        _row_copy(eo_hbm, pb_ref[base + r], buf_b, r, sem.at[1]).start()
        return c

    def wait(r, c):
        _row_copy(eo_hbm, 0, buf_a, r, sem.at[0]).wait()
        _row_copy(eo_hbm, 0, buf_b, r, sem.at[1]).wait()
        return c

    lax.fori_loop(0, tc, start, 0)
    lax.fori_loop(0, tc, wait, 0)
    y = ga_ref[...] * buf_a[...].reshape(bs, tl, D) + gb_ref[...] * buf_b[...].reshape(bs, tl, D)
    hn = h_ref[...] + gt_ref[...] * y
    out_ref[...] = hn * lax.rsqrt(jnp.mean(hn * hn, axis=-1, keepdims=True) + NORM_EPS) * g_ref[...]


def moe_combine_norm(eo, pos_a, pos_b, gate_a, gate_b, h, mod, gate_col, g, tok_offset, rows=128):
    n_seq, L, D = h.shape
    bs, tl = _seq_tiles(n_seq, L, rows)
    nl = L // tl
    blk = pl.BlockSpec((bs, tl, D), lambda i, pa, pb: (i // nl, i % nl, 0))
    wt = pl.BlockSpec((bs, tl, 1), lambda i, pa, pb: (i // nl, i % nl, 0))
    gs = pltpu.PrefetchScalarGridSpec(
        num_scalar_prefetch=2, grid=((n_seq * L) // (bs * tl),),
        in_specs=[pl.BlockSpec(memory_space=pl.ANY), blk,
                  pl.BlockSpec((bs, 1, D), lambda i, pa, pb: (i // nl, 0, gate_col)), wt, wt,
                  pl.BlockSpec((1, D), lambda i, pa, pb: (0, 0))],
        out_specs=blk,
        scratch_shapes=[pltpu.VMEM((bs * tl, D), F32), pltpu.VMEM((bs * tl, D), F32),
                        pltpu.SemaphoreType.DMA((2,))],
    )
    return pl.pallas_call(
        functools.partial(_combine_kernel, tok_offset=tok_offset), grid_spec=gs,
        out_shape=jax.ShapeDtypeStruct((n_seq, L, D), F32),
        compiler_params=_cparams(1), name="moe_combine_norm",
    )(pos_a, pos_b, eo, h, mod, gate_a, gate_b, g.reshape(1, D))


def _moe_plan(gate, sel, tm):
    T = gate.shape[0]
    P = TOP_K * T + N_EXPERTS * tm
    sel8 = sel[:, :N_EXPERTS]
    gate8 = gate[:, :N_EXPERTS]
    cnt = jnp.sum(sel8, axis=0)
    rank = jnp.cumsum(sel8, axis=0) - sel8
    cnt_pad = ((cnt + tm - 1) // tm) * tm
    ends = jnp.cumsum(cnt_pad)
    pos = (ends - cnt_pad)[None, :] + rank
    chosen = sel8 > 0
    pos_a = jnp.min(jnp.where(chosen, pos, P), axis=1).astype(jnp.int32)
    pos_b = jnp.max(jnp.where(chosen, pos, -1), axis=1).astype(jnp.int32)
    gate_a = jnp.sum(jnp.where(chosen & (pos == pos_a[:, None]), gate8, 0.0), axis=1)
    gate_b = jnp.sum(jnp.where(chosen & (pos == pos_b[:, None]), gate8, 0.0), axis=1)
    n_tiles = P // tm
    tile_start = jnp.arange(n_tiles, dtype=jnp.int32) * tm
    te = jnp.sum(tile_start[:, None] >= ends[None, :], axis=1).astype(jnp.int32)
    last_valid = jnp.maximum(ends[-1] // tm - 1, 0)
    te = jnp.where(tile_start < ends[-1], te, te[last_valid])
    te = jnp.minimum(te, N_EXPERTS - 1).astype(jnp.int32)
    tf = jnp.concatenate([jnp.ones((1,), jnp.int32), (te[1:] != te[:-1]).astype(jnp.int32)])
    return P, pos_a, pos_b, gate_a, gate_b, te, tf


def _layer0(h, mod, pos, w, caches):
    n_seq, L, D = h.shape
    cos, sin = _rope_tables(pos)
    n_g = len(A_GROUPS)
    xn = norm_mod(h, w['norm_mix_g'][0], mod, 0, 1)
    qkv = proj_rope(xn, w['a_w_qkv'][0], n_seq, L, cos, sin, lambda j: (j % 3) != 2, tn=A_COLS, name="a_qkv")
    if caches is None:
        outs, lses = zip(*[(band_attention_a if dil == 1 else dilated_attention_a)(qkv, g, dil)
                           for g, (_, dil) in enumerate(A_GROUPS)])
        h = merge_wo(outs, lses, w['a_w_o'][0], h, mod, 2)
        kv_new = [qkv[:, :, g * 3 * A_COLS + A_COLS:(g + 1) * 3 * A_COLS] for g in range(n_g)]
    else:
        tiles = qkv.reshape(n_seq, L, n_g, 3, A_HEADS, HEAD_DIM)
        merged = sample_attention_a(tiles.reshape(n_seq, L * n_g * 3 * A_HEADS, HEAD_DIM), caches)
        h = proj_residual(merged.reshape(n_seq * L, A_COLS).astype(BF16), w['a_w_o'][0], h, mod, 2, name="a_out")
        kv_new = [tiles[:, :, g, 1:].reshape(n_seq, L * 2 * A_HEADS, HEAD_DIM) for g in range(n_g)]
    xn = norm_mod(h, w['norm_ffn_g'][0], mod, 3, 4)
    act = swiglu_in(xn, w['ffn_w_in'][0], D_FF, name="ffn_in")
    h = proj_residual(act, w['ffn_w_out'][0], h, mod, 5, name="ffn_out")
    return h, kv_new


def _layer1_mixer(h, mod, kvmod, pos, w, past):
    n_seq, L, D = h.shape
    cos, sin = _rope_tables(pos)
    z = norm_mod(h, w['kv_norm_g'], kvmod, 0, 1)
    kvm = proj_rope(z, w['b_w_kv'], n_seq, L, cos, sin, lambda j: (j == 2) | (j == 4), tn=CMP_PLANE, name="b_kv")
    rows, win = kvm[:, :, :4 * CMP_PLANE], kvm[:, :, 4 * CMP_PLANE:]
    xn = norm_mod(h, w['norm_mix_g'][1], mod, 0, 1)
    n_q = B_HEADS * HEAD_DIM
    q = proj_rope(xn, w['b_w_qg'][0], n_seq, L, cos, sin, lambda j: j >= 0, tn=512, n_cols=n_q, name="b_q")
    wg = jnp.pad(w['b_w_qg'][0][:, n_q:], ((0, 0), (0, LANES - N_BRANCH * B_HEADS)))
    gates = proj_rope(xn, wg, n_seq, L, cos, sin, None, tn=LANES, sigmoid_out=True, name="b_gates")
    row_tiles = rows.reshape(n_seq, L, 4 * B_KV_HEADS, HEAD_DIM)
    if past is None:
        pages = row_tiles.reshape(n_seq * PAGES_PER_SEQ, PAGE_SIZE, 4 * B_KV_HEADS, HEAD_DIM)
        table = jnp.arange(n_seq * PAGES_PER_SEQ, dtype=jnp.int32).reshape(n_seq, PAGES_PER_SEQ)
        kc, vc = compress_pages(pages, table, w)
        gr = gates[:, :, :N_BRANCH * B_HEADS].reshape(n_seq, L, B_KV_HEADS, B_HPG * N_BRANCH).transpose(0, 2, 1, 3)
        o = nsa_prompt(q, gr, kc, vc, rows, win)
    else:
        pages, table, win_buf, past_len = past
        kc, vc = compress_pages(pages, table, w)
        o = nsa_sample(q, gates, kc, vc, pages, table, rows, win_buf, win, past_len)
    h = proj_residual(o.reshape(n_seq * L, n_q).astype(BF16), w['b_w_o'][0], h, mod, 2, name="b_out")
    return h, row_tiles, win


def kernel(x_prompt, x_sample, c_prompt, c_sample, cache_a_w128, cache_a_w512, cache_a_w2048, cache_b_kv,
           cache_b_win, page_table, norm_mix_g, norm_ffn_g, mod_w, mod_b, a_w_qkv, a_w_o, kv_norm_g, kv_mod_w,
           kv_mod_b, b_w_kv, cmp_w1_k, cmp_w2_k, cmp_pe_k, cmp_w1_v, cmp_w2_v, cmp_pe_v, b_w_qg, b_w_o,
           ffn_w_in, ffn_w_out, moe_w_router, moe_b_router, moe_w_in, moe_w_out, final_norm_g):
    w = {'norm_mix_g': norm_mix_g, 'norm_ffn_g': norm_ffn_g, 'a_w_qkv': a_w_qkv, 'a_w_o': a_w_o,
         'kv_norm_g': kv_norm_g, 'b_w_kv': b_w_kv, 'cmp_w1_k': cmp_w1_k, 'cmp_w2_k': cmp_w2_k,
         'cmp_pe_k': cmp_pe_k, 'cmp_w1_v': cmp_w1_v, 'cmp_w2_v': cmp_w2_v, 'cmp_pe_v': cmp_pe_v,
         'b_w_qg': b_w_qg, 'b_w_o': b_w_o, 'ffn_w_in': ffn_w_in, 'ffn_w_out': ffn_w_out}
    B, L, D = x_prompt.shape
    S, Q, _ = x_sample.shape
    past_len = page_table.shape[1] * PAGE_SIZE
    caches_a = (cache_a_w128, cache_a_w512, cache_a_w2048)

    n_c = B + S
    c_all = jnp.pad(jnp.concatenate([c_prompt, c_sample], axis=0), ((0, (-n_c) % SUBLANES), (0, 0)))
    mods = [mod_proj(c_all, mod_w[l], mod_b[l]) for l in range(2)]
    kvmod = mod_proj(c_all, kv_mod_w, kv_mod_b)

    def split(m):
        return m[:B].reshape(B, 1, -1), m[B:n_c].reshape(S, 1, -1)

    mod_p, mod_s = zip(*[split(m) for m in mods])
    kvmod_p, kvmod_s = split(kvmod)
    pos_p = jnp.arange(L, dtype=jnp.int32)
    pos_s = past_len + jnp.arange(Q, dtype=jnp.int32)

    h_p, kv_p = _layer0(x_prompt, mod_p[0], pos_p, w, None)
    h_s, kv_s = _layer0(x_sample, mod_s[0], pos_s, w, caches_a)
    a_out = []
    tok = 2 * A_HEADS
    for g, (window, _) in enumerate(A_GROUPS):
        keep_p = min(window, L)
        a_p = kv_p[g][:, L - keep_p:].reshape(1, B, keep_p, 2, A_HEADS, HEAD_DIM)
        n_buf = caches_a[g].shape[2]
        assert n_buf == window and window >= Q
        a_s = shift_append(caches_a[g].reshape(S, n_buf * tok, HEAD_DIM), kv_s[g])
        a_out += [a_p, a_s.reshape(1, S, n_buf, 2, A_HEADS, HEAD_DIM)]

    pages = cache_b_kv.reshape(cache_b_kv.shape[0], PAGE_SIZE, 4 * B_KV_HEADS, HEAD_DIM)
    n_win = cache_b_win.shape[1]
    win_tok = 2 * B_KV_HEADS
    win_buf = cache_b_win.reshape(S, n_win, win_tok, HEAD_DIM)
    h_p, rows_p, win_p = _layer1_mixer(h_p, mod_p[1], kvmod_p, pos_p, w, None)
    h_s, rows_s, win_s = _layer1_mixer(h_s, mod_s[1], kvmod_s, pos_s, w, (pages, page_table, win_buf, past_len))
    bkv_p = rows_p.reshape(B, L, 4, B_KV_HEADS, HEAD_DIM)
    bkv_s = rows_s.reshape(S, Q, 4, B_KV_HEADS, HEAD_DIM)
    keep = min(B_WINDOW, L)
    bwin_p = win_p[:, L - keep:].reshape(B, keep, 2, B_KV_HEADS, HEAD_DIM)
    assert n_win == B_WINDOW and B_WINDOW >= Q
    bwin_s = shift_append(cache_b_win.reshape(S, n_win * win_tok, HEAD_DIM), win_s.reshape(S, Q * win_tok, HEAD_DIM))
    bwin_s = bwin_s.reshape(S, n_win, 2, B_KV_HEADS, HEAD_DIM)

    xn_p = norm_mod(h_p, norm_ffn_g[1], mod_p[1], 3, 4, out_dtype=F32)
    xn_s = norm_mod(h_s, norm_ffn_g[1], mod_s[1], 3, 4, out_dtype=F32)
    routed = [moe_router(x, moe_w_router[0], moe_b_router[0]) for x in (xn_p, xn_s)]
    gate, sel = [jnp.concatenate(parts, axis=0) for parts in zip(*routed)]
    P, pos_a, pos_b, gate_a, gate_b, te, tf = _moe_plan(gate, sel, MOE_TILE)
    x_sorted = moe_dispatch(xn_p, pos_a, pos_b, jnp.zeros((P, D), F32), 0)
    x_sorted = moe_dispatch(xn_s, pos_a, pos_b, x_sorted, B * L)
    eo = moe_experts(x_sorted, moe_w_in[0], moe_w_out[0], te, tf)
    y_p = moe_combine_norm(eo, pos_a, pos_b, gate_a[:B * L].reshape(B, L, 1), gate_b[:B * L].reshape(B, L, 1),
                           h_p, mod_p[1], 5, final_norm_g, 0)
    y_s = moe_combine_norm(eo, pos_a, pos_b, gate_a[B * L:].reshape(S, Q, 1), gate_b[B * L:].reshape(S, Q, 1),
                           h_s, mod_s[1], 5, final_norm_g, B * L)

    return (y_p, y_s, a_out[0], a_out[1], a_out[2], a_out[3], a_out[4], a_out[5], bkv_p, bkv_s, bwin_p, bwin_s)
```
